```python
import math
import jax, jax.numpy as jnp
from jax import lax
import numpy as np

D_MODEL = 2048
BATCH = 4
SEQ = 4096
DEPTH = 2

PLE_DIM = 256
D_FF = 4 * D_MODEL
NORM_EPS = 1e-6
CHUNK = 64

A_WIDTH = D_MODEL // 2
A_HEAD_DIM = 128
A_HEADS = A_WIDTH // A_HEAD_DIM

B_WIDTH = D_MODEL // 2
B_HEAD_DIM = 128
B_HEADS = B_WIDTH // B_HEAD_DIM
CONV_WIDTH = 4

S5_GROUP = 16
S5_GROUPS = D_MODEL // S5_GROUP
S5_STATE = 64
DT_MIN = 0.001
DT_MAX = 0.1

N_EVEN = (DEPTH + 1) // 2
N_ODD = DEPTH // 2

IN_SIZES = [A_WIDTH, A_WIDTH, A_WIDTH, A_WIDTH, 3 * B_WIDTH, B_WIDTH, B_HEADS, B_HEADS]
IN_COLS = sum(IN_SIZES)
IN_CUTS = [int(c) for c in np.cumsum(IN_SIZES)[:-1]]

kernel_name = "hgrn2_deltanet_s5_hybrid_trunk"


def rms_norm(x, g):
    xf = x.astype(jnp.float32)
    y = xf * lax.rsqrt(jnp.mean(xf * xf, axis=-1, keepdims=True) + NORM_EPS)
    return (y * g.astype(jnp.float32)).astype(x.dtype)


def l2_norm(t):
    return t * lax.rsqrt(jnp.sum(t * t, axis=-1, keepdims=True) + NORM_EPS)


def to_head_chunks(t, heads):
    b, s, _ = t.shape
    return t.reshape(b, s // CHUNK, CHUNK, heads, -1).transpose(0, 3, 1, 2, 4)


def from_head_chunks(t):
    b, h, n, c, d = t.shape
    return t.transpose(0, 2, 3, 1, 4).reshape(b, n * c, h * d)


def to_scalar_chunks(t):
    b, s, h = t.shape
    return t.reshape(b, s // CHUNK, CHUNK, h).transpose(0, 3, 1, 2)


def causal_depthwise_conv(x, w):
    ch = x.shape[-1]
    return lax.conv_general_dilated(
        x, w.astype(x.dtype)[:, None, :], window_strides=(1,),
        padding=((CONV_WIDTH - 1, 0),), dimension_numbers=('NWC', 'WIO', 'NWC'),
        feature_group_count=ch)


def hgrn2_recurrence(q, k, v, log_f):
    bsz, heads, _, _, dk = q.shape
    dv = v.shape[-1]
    cum = jnp.cumsum(log_f, axis=3)
    cum_end = cum[:, :, :, -1:, :]
    q_dec = q * jnp.exp(cum)
    k_dec = k * jnp.exp(cum_end - cum)
    chunk_decay = jnp.exp(cum_end[:, :, :, 0, :])
    causal = jnp.tril(jnp.ones((CHUNK, CHUNK), dtype=bool))[:, :, None]

    def chunk_step(state, inputs):
        q_c, k_c, v_c, cum_c, qd_c, kd_c, dec_c = inputs
        rel = cum_c[:, :, :, None, :] - cum_c[:, :, None, :, :]
        pair_decay = jnp.exp(jnp.where(causal, rel, -jnp.inf))
        scores = jnp.einsum('bhtd,bhsd,bhtsd->bhts', q_c, k_c, pair_decay)
        out = (jnp.einsum('bhts,bhsv->bhtv', scores, v_c)
               + jnp.einsum('bhtd,bhdv->bhtv', qd_c, state))
        state = state * dec_c[..., None] + jnp.einsum('bhsd,bhsv->bhdv', kd_c, v_c)
        return state, out

    state0 = jnp.zeros((bsz, heads, dk, dv), jnp.float32)
    xs = tuple(jnp.moveaxis(t, 2, 0) for t in (q, k, v, cum, q_dec, k_dec, chunk_decay))
    _, out = lax.scan(chunk_step, state0, xs)
    return jnp.moveaxis(out, 0, 2)


def gated_delta_rule(q, k, v, log_a, beta):
    bsz, heads, _, _, dk = q.shape
    dv = v.shape[-1]
    cum = jnp.cumsum(log_a, axis=-1)
    causal = jnp.tril(jnp.ones((CHUNK, CHUNK), dtype=bool))
    strict = jnp.tril(jnp.ones((CHUNK, CHUNK), dtype=bool), k=-1)
    decay = jnp.exp(jnp.where(causal, cum[..., :, None] - cum[..., None, :], -jnp.inf))
    k_beta = k * beta[..., None]
    lower = jnp.where(strict, jnp.einsum('bhntd,bhnsd->bhnts', k_beta, k) * decay, 0.0)
    rhs = jnp.concatenate([v * beta[..., None], k_beta * jnp.exp(cum)[..., None]], axis=-1)
    sol = lax.linalg.triangular_solve(lower + jnp.eye(CHUNK, dtype=lower.dtype), rhs,
                                      left_side=True, lower=True, unit_diagonal=True)
    u, w = sol[..., :dv], sol[..., dv:]
    intra = jnp.einsum('bhntd,bhnsd->bhnts', q, k) * decay
    q_dec = q * jnp.exp(cum)[..., None]
    k_dec = k * jnp.exp(cum[..., -1:] - cum)[..., None]
    chunk_decay = jnp.exp(cum[..., -1])

    def chunk_step(state, inputs):
        qd_c, kd_c, u_c, w_c, a_c, dec_c = inputs
        v_new = u_c - jnp.einsum('bhcd,bhdv->bhcv', w_c, state)
        out = (jnp.einsum('bhcd,bhdv->bhcv', qd_c, state)
               + jnp.einsum('bhts,bhsv->bhtv', a_c, v_new))
        state = state * dec_c[..., None, None] + jnp.einsum('bhsd,bhsv->bhdv', kd_c, v_new)
        return state, out

    state0 = jnp.zeros((bsz, heads, dk, dv), jnp.float32)
    xs = tuple(jnp.moveaxis(t, 2, 0) for t in (q_dec, k_dec, u, w, intra, chunk_decay))
    _, out = lax.scan(chunk_step, state0, xs)
    return jnp.moveaxis(out, 0, 2)


def hgrn2_deltanet_mixer(hn, w_in, w_out, lb, g_norm_a, conv_w, a_log, dt_bias, g_norm_b):
    f32 = jnp.float32
    proj = hn @ w_in
    q_a, f_a, i_a, g_a, qkv_b, z_b, a_b, b_b = jnp.split(proj, IN_CUTS, axis=-1)

    forget = lb + (1.0 - lb) * jax.nn.sigmoid(f_a.astype(f32))
    o_a = hgrn2_recurrence(to_head_chunks(q_a.astype(f32), A_HEADS),
                           to_head_chunks(1.0 - forget, A_HEADS),
                           to_head_chunks(i_a.astype(f32), A_HEADS),
                           to_head_chunks(jnp.log(forget), A_HEADS))
    o_a = from_head_chunks(rms_norm(o_a, g_norm_a)) * jax.nn.silu(g_a.astype(f32))

    qkv = jax.nn.silu(causal_depthwise_conv(qkv_b, conv_w)).astype(f32)
    q_b, k_b, v_b = jnp.split(qkv, 3, axis=-1)
    q_b = l2_norm(to_head_chunks(q_b, B_HEADS)) * (B_HEAD_DIM ** -0.5)
    k_b = l2_norm(to_head_chunks(k_b, B_HEADS))
    v_b = to_head_chunks(v_b, B_HEADS)
    beta = jax.nn.sigmoid(to_scalar_chunks(b_b.astype(f32)))
    log_a = -jnp.exp(a_log.astype(f32)) * jax.nn.softplus(a_b.astype(f32) + dt_bias.astype(f32))
    o_b = gated_delta_rule(q_b, k_b, v_b, to_scalar_chunks(log_a), beta)
    o_b = from_head_chunks(rms_norm(o_b, g_norm_b)) * jax.nn.silu(z_b.astype(f32))

    merged = jnp.concatenate([o_a, o_b], axis=-1).astype(hn.dtype)
    return merged @ w_out


def ssm_combine(e1, e2):
    a1, b1 = e1
    a2, b2 = e2
    return a1 * a2, a2 * b1 + b2


def s5_mixer(hn, a_re, a_im, b_re, b_im, c_re, c_im, d_skip, log_dt, w_glu, b_glu, w_out):
    f32 = jnp.float32
    bsz, s, _ = hn.shape
    u = hn.astype(f32)
    u_g = u.reshape(bsz, s, S5_GROUPS, S5_GROUP).astype(jnp.complex64)
    lam = lax.complex(a_re.astype(f32), a_im.astype(f32))
    step = jnp.exp(log_dt.astype(f32))[:, None]
    lam_bar = jnp.exp(lam * step)
    b_mat = lax.complex(b_re.astype(f32), b_im.astype(f32))
    b_bar = ((lam_bar - 1.0) / lam)[..., None] * b_mat
    c_mat = lax.complex(c_re.astype(f32), c_im.astype(f32))
    bu = jnp.einsum('gpc,bsgc->bsgp', b_bar, u_g)
    a_elems = jnp.broadcast_to(lam_bar, (1, s) + lam_bar.shape)
    _, states = lax.associative_scan(ssm_combine, (a_elems, bu), axis=1)
    y = jnp.einsum('gcp,bsgp->bsgc', c_mat, states).real.reshape(bsz, s, D_MODEL)
    y = y + d_skip.astype(f32) * u
    act = jax.nn.gelu(y)
    glu = act * jax.nn.sigmoid(act @ w_glu.astype(f32) + b_glu.astype(f32))
    return glu.astype(hn.dtype) @ w_out


def setup_inputs(seed: int = 0) -> dict:
    key = jax.random.key(seed)
    ks = iter(jax.random.split(key, 40))
    f32 = jnp.float32

    def nrm(shape, scale):
        return scale * jax.random.normal(next(ks), shape, f32)

    def unif(shape, lo, hi):
        return jax.random.uniform(next(ks), shape, f32, minval=lo, maxval=hi)

    x = nrm((BATCH, SEQ, D_MODEL), 1.0)
    p = nrm((DEPTH, BATCH, SEQ, PLE_DIM), 1.0)
    norm_mix = 1.0 + nrm((DEPTH, D_MODEL), 0.02)
    norm_mlp = 1.0 + nrm((DEPTH, D_MODEL), 0.02)
    norm_ple = 1.0 + nrm((DEPTH, D_MODEL), 0.02)
    w_in_e = nrm((N_EVEN, D_MODEL, IN_COLS), D_MODEL ** -0.5)
    w_out_e = nrm((N_EVEN, D_MODEL, D_MODEL), D_MODEL ** -0.5)
    hgrn_lb = nrm((DEPTH + 1, A_WIDTH), 0.1)
    g_norm_a = 1.0 + nrm((N_EVEN, A_HEAD_DIM), 0.02)
    conv_w = nrm((N_EVEN, CONV_WIDTH, 3 * B_WIDTH), CONV_WIDTH ** -0.5)
    a_log = jnp.log(unif((N_EVEN, B_HEADS), 1.0, 16.0))
    dt = jnp.exp(unif((N_EVEN, B_HEADS), math.log(DT_MIN), math.log(DT_MAX)))
    dt_bias = dt + jnp.log(-jnp.expm1(-dt))
    g_norm_b = 1.0 + nrm((N_EVEN, B_HEAD_DIM), 0.02)
    s5_a_re = -0.5 + nrm((N_ODD, S5_GROUPS, S5_STATE), 0.01)
    s5_a_im = (math.pi * jnp.arange(S5_STATE, dtype=f32))[None, None, :] + nrm((N_ODD, S5_GROUPS, S5_STATE), 0.01)
    s5_b_re = nrm((N_ODD, S5_GROUPS, S5_STATE, S5_GROUP), (2 * S5_GROUP) ** -0.5)
    s5_b_im = nrm((N_ODD, S5_GROUPS, S5_STATE, S5_GROUP), (2 * S5_GROUP) ** -0.5)
    s5_c_re = nrm((N_ODD, S5_GROUPS, S5_GROUP, S5_STATE), S5_STATE ** -0.5)
    s5_c_im = nrm((N_ODD, S5_GROUPS, S5_GROUP, S5_STATE), S5_STATE ** -0.5)
    s5_d = nrm((N_ODD, D_MODEL), 1.0)
    s5_log_dt = unif((N_ODD, S5_GROUPS), math.log(DT_MIN), math.log(DT_MAX))
    w_glu = nrm((N_ODD, D_MODEL, D_MODEL), D_MODEL ** -0.5)
    b_glu = nrm((N_ODD, D_MODEL), 0.01)
    w_out_o = nrm((N_ODD, D_MODEL, D_MODEL), D_MODEL ** -0.5)
    w_up = nrm((DEPTH, D_MODEL, D_FF), D_MODEL ** -0.5)
    w_down = nrm((DEPTH, D_FF, D_MODEL), D_FF ** -0.5)
    w_ple_gate = nrm((DEPTH, D_MODEL, D_MODEL), D_MODEL ** -0.5)
    w_ple_proj = nrm((DEPTH, PLE_DIM, D_MODEL), PLE_DIM ** -0.5)
    final_norm = 1.0 + nrm((D_MODEL,), 0.02)
    return {"x": x, "p": p, "norm_mix": norm_mix, "norm_mlp": norm_mlp, "norm_ple": norm_ple,
            "w_in_e": w_in_e, "w_out_e": w_out_e, "hgrn_lb": hgrn_lb, "g_norm_a": g_norm_a,
            "conv_w": conv_w, "a_log": a_log, "dt_bias": dt_bias, "g_norm_b": g_norm_b,
            "s5_a_re": s5_a_re, "s5_a_im": s5_a_im, "s5_b_re": s5_b_re, "s5_b_im": s5_b_im,
            "s5_c_re": s5_c_re, "s5_c_im": s5_c_im, "s5_d": s5_d, "s5_log_dt": s5_log_dt,
            "w_glu": w_glu, "b_glu": b_glu, "w_out_o": w_out_o, "w_up": w_up, "w_down": w_down,
            "w_ple_gate": w_ple_gate, "w_ple_proj": w_ple_proj, "final_norm": final_norm}


def reference(x, p, norm_mix, norm_mlp, norm_ple, w_in_e, w_out_e, hgrn_lb, g_norm_a,
              conv_w, a_log, dt_bias, g_norm_b, s5_a_re, s5_a_im, s5_b_re, s5_b_im,
              s5_c_re, s5_c_im, s5_d, s5_log_dt, w_glu, b_glu, w_out_o, w_up, w_down,
              w_ple_gate, w_ple_proj, final_norm):
    lower_bounds = jnp.cumsum(jax.nn.softmax(hgrn_lb.astype(jnp.float32), axis=0), axis=0)
    h = x
    for i in range(DEPTH):
        j = i // 2
        hn = rms_norm(h, norm_mix[i])
        if i % 2 == 0:
            mix = hgrn2_deltanet_mixer(hn, w_in_e[j], w_out_e[j], lower_bounds[i], g_norm_a[j],
                                       conv_w[j], a_log[j], dt_bias[j], g_norm_b[j])
        else:
            mix = s5_mixer(hn, s5_a_re[j], s5_a_im[j], s5_b_re[j], s5_b_im[j], s5_c_re[j],
                           s5_c_im[j], s5_d[j], s5_log_dt[j], w_glu[j], b_glu[j], w_out_o[j])
        h = h + mix.astype(h.dtype)
        hn = rms_norm(h, norm_mlp[i])
        h = h + (jnp.square(jax.nn.relu(hn @ w_up[i])) @ w_down[i]).astype(h.dtype)
        gate = jax.nn.sigmoid(rms_norm(h, norm_ple[i]) @ w_ple_gate[i])
        h = h + (gate * (p[i] @ w_ple_proj[i])).astype(h.dtype)
    return rms_norm(h, final_norm)
```

```python
import functools
import math

import jax
import jax.numpy as jnp
from jax import lax
from jax.experimental import pallas as pl
from jax.experimental.pallas import tpu as pltpu

F32 = jnp.float32
BF16 = jnp.bfloat16

NORM_EPS = 1e-6
CHUNK = 64
HEAD_DIM = 128
N_HEADS = 8
CONV_WIDTH = 4
S5_GROUP = 16
S5_STATE = 64
S5_L = 16
S5_W = S5_L * S5_GROUP
EXP_CLAMP = 80.0

VMEM_LIMIT = 56 * 1024 * 1024


def _sigmoid(x):
    return 1.0 / (1.0 + jnp.exp(-x))


def _silu(x):
    return x * _sigmoid(x)


def _rms(x, g):
    return x * lax.rsqrt(jnp.mean(x * x, axis=-1, keepdims=True) + NORM_EPS) * g


def _dot(a, b):
    return jnp.dot(a, b, preferred_element_type=F32)


def _dot_nt(a, b):
    return lax.dot_general(a, b, (((1,), (1,)), ((), ())), preferred_element_type=F32)


def _dot_tn(a, b):
    return lax.dot_general(a, b, (((0,), (0,)), ((), ())), preferred_element_type=F32)


def _dot_split(a_bf, x):
    hi = x.astype(BF16)
    lo = (x - hi.astype(F32)).astype(BF16)
    return _dot(a_bf, hi) + _dot(a_bf, lo)


def _params(*sem):
    return pltpu.CompilerParams(dimension_semantics=sem, vmem_limit_bytes=VMEM_LIMIT)


def _in_proj_kernel(x_ref, g_ref, w_ref, ws_ref, o_ref, os_ref, hn_ref):
    @pl.when(pl.program_id(1) == 0)
    def _():
        hn = _rms(x_ref[...], g_ref[...]).astype(BF16)
        hn_ref[...] = hn
        os_ref[...] = _dot(hn, ws_ref[...])

    o_ref[...] = _dot(hn_ref[...], w_ref[...])


def _in_proj(x, g, w, w_small, *, tm=512, tn=512):
    t, d = x.shape
    n = w.shape[1]
    ns = w_small.shape[1]
    return pl.pallas_call(
        _in_proj_kernel,
        out_shape=(jax.ShapeDtypeStruct((t, n), F32), jax.ShapeDtypeStruct((t, ns), F32)),
        grid=(t // tm, n // tn),
        in_specs=[pl.BlockSpec((tm, d), lambda i, j: (i, 0)),
                  pl.BlockSpec((1, d), lambda i, j: (0, 0)),
                  pl.BlockSpec((d, tn), lambda i, j: (0, j)),
                  pl.BlockSpec((d, ns), lambda i, j: (0, 0))],
        out_specs=(pl.BlockSpec((tm, tn), lambda i, j: (i, j)),
                   pl.BlockSpec((tm, ns), lambda i, j: (i, 0))),
        scratch_shapes=[pltpu.VMEM((tm, d), BF16)],
        compiler_params=_params("parallel", "arbitrary"),
        name="in_proj",
    )(x, g, w, w_small)


def _mm_resid_kernel(*refs, n_in):
    x_refs = refs[:n_in]
    w_refs = refs[n_in:2 * n_in]
    r_ref = refs[2 * n_in]
    o_ref = refs[2 * n_in + 1]
    acc = r_ref[...]
    for x_ref, w_ref in zip(x_refs, w_refs):
        acc = acc + _dot(x_ref[...], w_ref[...])
    o_ref[...] = acc


def _mm_resid(xs, ws, resid, *, tm=512, tn=512):
    t, n = resid.shape
    n_in = len(xs)
    in_specs = ([pl.BlockSpec((tm, x.shape[1]), lambda i, j: (i, 0)) for x in xs]
                + [pl.BlockSpec((w.shape[0], tn), lambda i, j: (0, j)) for w in ws]
                + [pl.BlockSpec((tm, tn), lambda i, j: (i, j))])
    return pl.pallas_call(
        functools.partial(_mm_resid_kernel, n_in=n_in),
        out_shape=jax.ShapeDtypeStruct((t, n), F32),
        grid=(t // tm, n // tn),
        in_specs=in_specs,
        out_specs=pl.BlockSpec((tm, tn), lambda i, j: (i, j)),
        compiler_params=_params("parallel", "arbitrary"),
        name="mm_resid",
    )(*xs, *ws, resid)


def _mlp_kernel(h_ref, g_ref, wu_ref, wd_ref, o_ref, hn_ref):
    @pl.when(pl.program_id(1) == 0)
    def _():
        h = h_ref[...]
        hn_ref[...] = _rms(h, g_ref[...]).astype(BF16)
        o_ref[...] = h

    a = jnp.maximum(_dot(hn_ref[...], wu_ref[...]), 0.0)
    o_ref[...] += _dot((a * a).astype(BF16), wd_ref[...])


def _mlp(h, g, w_up, w_down, *, tm=512, tf=512):
    t, d = h.shape
    f = w_up.shape[1]
    return pl.pallas_call(
        _mlp_kernel,
        out_shape=jax.ShapeDtypeStruct((t, d), F32),
        grid=(t // tm, f // tf),
        in_specs=[pl.BlockSpec((tm, d), lambda i, j: (i, 0)),
                  pl.BlockSpec((1, d), lambda i, j: (0, 0)),
                  pl.BlockSpec((d, tf), lambda i, j: (0, j)),
                  pl.BlockSpec((tf, d), lambda i, j: (j, 0))],
        out_specs=pl.BlockSpec((tm, d), lambda i, j: (i, 0)),
        scratch_shapes=[pltpu.VMEM((tm, d), BF16)],
        compiler_params=_params("parallel", "arbitrary"),
        name="mlp",
    )(h, g, w_up, w_down)


def _ple_kernel(h_ref, g_ref, wg_ref, p_ref, wp_ref, g2_ref, o_ref, on_ref, hn_ref, *, tn):
    j = pl.program_id(1)

    @pl.when(j == 0)
    def _():
        hn_ref[...] = _rms(h_ref[...], g_ref[...]).astype(BF16)

    col = pl.multiple_of(j * tn, tn)
    gate = _sigmoid(_dot(hn_ref[...], wg_ref[...]))
    emb = _dot(p_ref[...], wp_ref[...])
    o_ref[:, pl.ds(col, tn)] = h_ref[:, pl.ds(col, tn)] + gate * emb

    @pl.when(j == pl.num_programs(1) - 1)
    def _():
        on_ref[...] = _rms(o_ref[...], g2_ref[...])


def _ple(h, g, w_gate, p, w_proj, g_next, *, tm=512, tn=512):
    t, d = h.shape
    pd = p.shape[1]
    return pl.pallas_call(
        functools.partial(_ple_kernel, tn=tn),
        out_shape=(jax.ShapeDtypeStruct((t, d), F32), jax.ShapeDtypeStruct((t, d), F32)),
        grid=(t // tm, d // tn),
        in_specs=[pl.BlockSpec((tm, d), lambda i, j: (i, 0)),
                  pl.BlockSpec((1, d), lambda i, j: (0, 0)),
                  pl.BlockSpec((d, tn), lambda i, j: (0, j)),
                  pl.BlockSpec((tm, pd), lambda i, j: (i, 0)),
                  pl.BlockSpec((pd, tn), lambda i, j: (0, j)),
                  pl.BlockSpec((1, d), lambda i, j: (0, 0))],
        out_specs=(pl.BlockSpec((tm, d), lambda i, j: (i, 0)),
                   pl.BlockSpec((tm, d), lambda i, j: (i, 0))),
        scratch_shapes=[pltpu.VMEM((tm, d), BF16)],
        compiler_params=_params("parallel", "arbitrary"),
        name="ple",
    )(h, g, w_gate, p, w_proj, g_next)


def _glu_kernel(a_ref, w_ref, b_ref, o_ref, abf_ref, *, tn):
    j = pl.program_id(1)

    @pl.when(j == 0)
    def _():
        abf_ref[...] = a_ref[...].astype(BF16)

    col = pl.multiple_of(j * tn, tn)
    z = _dot(abf_ref[...], w_ref[...]) + b_ref[...]
    o_ref[...] = (a_ref[:, pl.ds(col, tn)] * _sigmoid(z)).astype(BF16)


def _glu(act, w, b, *, tm=512, tn=512):
    t, d = act.shape
    return pl.pallas_call(
        functools.partial(_glu_kernel, tn=tn),
        out_shape=jax.ShapeDtypeStruct((t, d), BF16),
        grid=(t // tm, d // tn),
        in_specs=[pl.BlockSpec((tm, d), lambda i, j: (i, 0)),
                  pl.BlockSpec((d, tn), lambda i, j: (0, j)),
                  pl.BlockSpec((1, tn), lambda i, j: (0, j))],
        out_specs=pl.BlockSpec((tm, tn), lambda i, j: (i, j)),
        scratch_shapes=[pltpu.VMEM((tm, d), BF16)],
        compiler_params=_params("parallel", "arbitrary"),
        name="glu",
    )(act, w, b)


def _row(x, t):
    return x[t:t + 1, :]


def _hgrn_kernel(q_ref, f_ref, i_ref, g_ref, lb_ref, gn_ref, o_ref, st_ref, *, n_chunks):
    c_len, d = CHUNK, HEAD_DIM

    @pl.when(pl.program_id(2) == 0)
    def _():
        st_ref[...] = jnp.zeros_like(st_ref)

    ri = lax.broadcasted_iota(jnp.int32, (c_len, d), 0)
    rt = lax.broadcasted_iota(jnp.int32, (c_len, c_len), 0)
    rs = lax.broadcasted_iota(jnp.int32, (c_len, c_len), 1)
    tril = (rs <= rt).astype(BF16)
    diag_mask = jnp.logical_and(rs <= rt, (rs // 16) == (rt // 16))
    lb = lb_ref[...]
    gn = gn_ref[...]

    def chunk(c, carry):
        r0 = pl.multiple_of(c * c_len, c_len)
        q = q_ref[pl.ds(r0, c_len), :]
        forget = lb + (1.0 - lb) * _sigmoid(f_ref[pl.ds(r0, c_len), :])
        k = 1.0 - forget
        v = i_ref[pl.ds(r0, c_len), :].astype(BF16)
        cum = _dot_split(tril, jnp.log(forget))

        def side(valid, ref_row, sign, x):
            e = jnp.where(valid, sign * (cum - ref_row), 0.0)
            return jnp.where(valid, x * jnp.exp(e), 0.0)

        c31 = _row(cum, 31)
        ref_b = jnp.where(ri < 32, _row(cum, 15), _row(cum, 47))
        ref_d = jnp.where(ri < 16, _row(cum, 8),
                          jnp.where(ri < 32, _row(cum, 24),
                                    jnp.where(ri < 48, _row(cum, 40), _row(cum, 56))))
        hi16 = (ri % 32) >= 16
        q_b = side(hi16, ref_b, 1.0, q)
        k_b = side(jnp.logical_not(hi16), ref_b, -1.0, k)
        q_off = jnp.concatenate([side(ri >= 32, c31, 1.0, q),
                                 jnp.where(ri < 32, q_b, 0.0),
                                 jnp.where(ri >= 32, q_b, 0.0)], axis=1).astype(BF16)
        k_off = jnp.concatenate([side(ri < 32, c31, -1.0, k),
                                 jnp.where(ri < 32, k_b, 0.0),
                                 jnp.where(ri >= 32, k_b, 0.0)], axis=1).astype(BF16)
        q_d = (q * jnp.exp(jnp.minimum(cum - ref_d, EXP_CLAMP))).astype(BF16)
        k_d = (k * jnp.exp(jnp.minimum(ref_d - cum, EXP_CLAMP))).astype(BF16)
        scores = _dot_nt(q_off, k_off) + jnp.where(diag_mask, _dot_nt(q_d, k_d), 0.0)

        cum_end = _row(cum, c_len - 1)
        q_dec = (q * jnp.exp(cum)).astype(BF16)
        k_dec = (k * jnp.exp(cum_end - cum)).astype(BF16)
        st = st_ref[...]
        out = _dot(scores.astype(BF16), v) + _dot_nt(q_dec, st.astype(BF16))
        st_ref[...] = st * jnp.exp(cum_end) + _dot_tn(v, k_dec)

        out = _rms(out, gn) * _silu(g_ref[pl.ds(r0, c_len), :])
        o_ref[pl.ds(r0, c_len), :] = out.astype(o_ref.dtype)
        return carry

    lax.fori_loop(0, n_chunks, chunk, 0)


def _hgrn(proj, lb, g_norm, *, bsz, seq, sb=1024):
    sb = min(sb, seq)
    h = N_HEADS

    def spec(off):
        return pl.BlockSpec((None, sb, HEAD_DIM), lambda b, hh, s: (b, s, hh + off))

    return pl.pallas_call(
        functools.partial(_hgrn_kernel, n_chunks=sb // CHUNK),
        out_shape=jax.ShapeDtypeStruct((bsz, seq, h * HEAD_DIM), BF16),
        grid=(bsz, h, seq // sb),
        in_specs=[spec(0), spec(h), spec(2 * h), spec(3 * h),
                  pl.BlockSpec((None, 1, HEAD_DIM), lambda b, hh, s: (hh, 0, 0)),
                  pl.BlockSpec((1, HEAD_DIM), lambda b, hh, s: (0, 0))],
        out_specs=pl.BlockSpec((None, sb, HEAD_DIM), lambda b, hh, s: (b, s, hh)),
        scratch_shapes=[pltpu.VMEM((HEAD_DIM, HEAD_DIM), F32)],
        compiler_params=_params("parallel", "parallel", "arbitrary"),
        name="hgrn2",
    )(proj, proj, proj, proj, lb, g_norm)


def _delta_kernel(q_ref, k_ref, v_ref, z_ref, ab_ref, cw_ref, sc_ref, gn_ref, o_ref, st_ref,
                  *, n_chunks):
    c_len, d = CHUNK, HEAD_DIM
    head = pl.program_id(1)
    first_block = pl.program_id(2) == 0

    @pl.when(first_block)
    def _():
        st_ref[...] = jnp.zeros_like(st_ref)

    rt = lax.broadcasted_iota(jnp.int32, (c_len, c_len), 0)
    rs = lax.broadcasted_iota(jnp.int32, (c_len, c_len), 1)
    causal = rs <= rt
    strict = rs < rt
    tril = causal.astype(BF16)
    ones_cc = jnp.ones((c_len, c_len), BF16)
    eye = (rs == rt).astype(F32)
    bt, bs = rt // 16, rs // 16
    m_diag = jnp.logical_and(strict, bt == bs)
    m_l1 = jnp.logical_and(bt // 2 == bs // 2, bt == bs + 1)
    m_l2 = jnp.logical_and(bt >= 2, bs < 2)
    lane = lax.broadcasted_iota(jnp.int32, (c_len, d), 1)
    cw = cw_ref[...]
    gn = gn_ref[...]
    neg_a = -jnp.exp(sc_ref[0:1, :])
    dt_bias = sc_ref[1:2, :]

    def conv(ref, which, c, r0):
        prev0 = pl.multiple_of(jnp.maximum(r0 - 8, 0), 8)
        has_prev = jnp.logical_or(c > 0, jnp.logical_not(first_block))
        prev = jnp.where(has_prev, ref[pl.ds(prev0, 8), :], 0.0)
        win = jnp.concatenate([prev, ref[pl.ds(r0, c_len), :]], axis=0)
        acc = win * cw[which, CONV_WIDTH - 1:CONV_WIDTH, :]
        for j in range(1, CONV_WIDTH):
            shifted = pltpu.roll(win, j, axis=0)
            acc = acc + shifted * cw[which, CONV_WIDTH - 1 - j:CONV_WIDTH - j, :]
        return _silu(acc[8:, :])

    def chunk(c, carry):
        r0 = pl.multiple_of(c * c_len, c_len)
        q = conv(q_ref, 0, c, r0)
        k = conv(k_ref, 1, c, r0)
        v = conv(v_ref, 2, c, r0)
        q = q * lax.rsqrt(jnp.sum(q * q, axis=-1, keepdims=True) + NORM_EPS) * (d ** -0.5)
        k = k * lax.rsqrt(jnp.sum(k * k, axis=-1, keepdims=True) + NORM_EPS)

        ab = ab_ref[pl.ds(r0, c_len), :]
        a_col = jnp.sum(jnp.where(lane == head, ab, 0.0), axis=-1, keepdims=True)
        b_col = jnp.sum(jnp.where(lane == head + N_HEADS, ab, 0.0), axis=-1, keepdims=True)
        beta = _sigmoid(b_col)
        x = a_col + dt_bias
        log_a = neg_a * (jnp.maximum(x, 0.0) + jnp.log(1.0 + jnp.exp(-jnp.abs(x))))
        cum = _dot_split(tril, log_a)
        cum_row = _dot_split(ones_cc, jnp.where(rs >= rt, log_a[:, :c_len], 0.0))
        decay = jnp.exp(jnp.where(causal, cum[:, :c_len] - cum_row, 0.0))

        k_bf = k.astype(BF16)
        k_beta = k * beta
        n_mat = jnp.where(strict, _dot_nt(k_beta.astype(BF16), k_bf) * decay, 0.0)

        def mm(a, b):
            return _dot(a.astype(BF16), b.astype(BF16))

        n_d = jnp.where(m_diag, n_mat, 0.0)
        p2 = mm(n_d, n_d)
        p4 = mm(p2, p2)
        p8 = mm(p4, p4)
        t_d = mm(mm(eye - n_d, eye + p2), mm(eye + p4, eye + p8))
        t_32 = t_d - mm(mm(t_d, jnp.where(m_l1, n_mat, 0.0)), t_d)
        t_inv = t_32 - mm(mm(t_32, jnp.where(m_l2, n_mat, 0.0)), t_32)

        e_cum = jnp.exp(cum)
        rhs = jnp.concatenate([v * beta, k_beta * e_cum], axis=1)
        sol = mm(t_inv, rhs)
        u, w = sol[:, :d], sol[:, d:]
        intra = jnp.where(causal, _dot_nt(q.astype(BF16), k_bf) * decay, 0.0)
        cum_end = _row(cum, c_len - 1)
        q_dec = (q * e_cum).astype(BF16)
        k_dec = (k * jnp.exp(cum_end - cum)).astype(BF16)

        st = st_ref[...]
        st_bf = st.astype(BF16)
        v_new = u - _dot(w.astype(BF16), st_bf)
        v_new_bf = v_new.astype(BF16)
        out = _dot(q_dec, st_bf) + _dot(intra.astype(BF16), v_new_bf)
        st_ref[...] = st * jnp.exp(cum_end) + _dot_tn(k_dec, v_new_bf)

        out = _rms(out, gn) * _silu(z_ref[pl.ds(r0, c_len), :])
        o_ref[pl.ds(r0, c_len), :] = out.astype(o_ref.dtype)
        return carry

    lax.fori_loop(0, n_chunks, chunk, 0)


def _delta(proj, small, conv_w, scal, g_norm, *, bsz, seq, col0):
    h = N_HEADS
    sb = seq

    def spec(off):
        return pl.BlockSpec((None, sb, HEAD_DIM), lambda b, hh, s: (b, s, hh + off))

    return pl.pallas_call(
        functools.partial(_delta_kernel, n_chunks=sb // CHUNK),
        out_shape=jax.ShapeDtypeStruct((bsz, seq, h * HEAD_DIM), BF16),
        grid=(bsz, h, seq // sb),
        in_specs=[spec(col0), spec(col0 + h), spec(col0 + 2 * h), spec(col0 + 3 * h),
                  pl.BlockSpec((None, sb, HEAD_DIM), lambda b, hh, s: (b, s, 0)),
                  pl.BlockSpec((None, 3, CONV_WIDTH, HEAD_DIM), lambda b, hh, s: (hh, 0, 0, 0)),
                  pl.BlockSpec((None, 2, HEAD_DIM), lambda b, hh, s: (hh, 0, 0)),
                  pl.BlockSpec((1, HEAD_DIM), lambda b, hh, s: (0, 0))],
        out_specs=pl.BlockSpec((None, sb, HEAD_DIM), lambda b, hh, s: (b, s, hh)),
        scratch_shapes=[pltpu.VMEM((HEAD_DIM, HEAD_DIM), F32)],
        compiler_params=_params("parallel", "parallel", "arbitrary"),
        name="deltanet",
    )(proj, proj, proj, proj, small, conv_w, scal, g_norm)


def _cmul(ar, ai, br, bi):
    return ar * br - ai * bi, ar * bi + ai * br


def _s5_prep_kernel(are_ref, aim_ref, ldt_ref, bre_ref, bim_ref, cre_ref, cim_ref,
                    are2_ref, aim2_ref, ldt2_ref,
                    toep_ref, m1_ref, m1s_ref, m2_ref, lam_ref):
    p, w, l, gs = S5_STATE, S5_W, S5_L, S5_GROUP
    hp = lax.Precision.HIGHEST
    dt = jnp.exp(ldt_ref[...])
    a_re = jnp.broadcast_to(are_ref[...], (p, w))
    a_im = jnp.broadcast_to(aim_ref[...], (p, w))
    lr, li = a_re * dt, a_im * dt
    mag = jnp.exp(lr)
    lb_re, lb_im = mag * jnp.cos(li), mag * jnp.sin(li)
    den = a_re * a_re + a_im * a_im
    xr, xi = lb_re - 1.0, lb_im
    coef_re, coef_im = (xr * a_re + xi * a_im) / den, (xi * a_re - xr * a_im) / den
    bb_re, bb_im = _cmul(coef_re, coef_im, bre_ref[...], bim_ref[...])

    kf = (lax.broadcasted_iota(jnp.int32, (p, w), 1) // gs).astype(F32)

    def lam_pow(e):
        m = jnp.exp(e * lr)
        return m * jnp.cos(e * li), m * jnp.sin(e * li)

    c_re, c_im = cre_ref[...], cim_ref[...]
    e_re, e_im = _cmul(*lam_pow(kf), c_re, c_im)
    lhs = jnp.concatenate([bb_re[:, :gs], -bb_im[:, :gs]], axis=0)
    rhs = jnp.concatenate([e_re, e_im], axis=0)
    r0 = lax.dot_general(lhs, rhs, (((0,), (0,)), ((), ())), precision=hp,
                         preferred_element_type=F32)
    lane = lax.broadcasted_iota(jnp.int32, (gs, w), 1)
    for s in range(l):
        blk = r0 if s == 0 else jnp.where(lane >= s * gs, pltpu.roll(r0, s * gs, axis=1), 0.0)
        toep_ref[s * gs:(s + 1) * gs, :] = blk.astype(toep_ref.dtype)

    d_re, d_im = _cmul(*lam_pow(float(l - 1) - kf), bb_re, bb_im)
    m1_ref[...] = jnp.concatenate([d_re, d_im], axis=0).astype(m1_ref.dtype)
    m1s_ref[...] = jnp.concatenate([d_im, d_re], axis=0).astype(m1s_ref.dtype)
    f_re, f_im = _cmul(*lam_pow(kf + 1.0), c_re, c_im)
    m2_ref[...] = jnp.concatenate([f_re, -f_im], axis=0).astype(m2_ref.dtype)

    dt2 = jnp.exp(ldt2_ref[...])
    m2_ = jnp.exp(float(l) * are2_ref[...] * dt2)
    ang = float(l) * aim2_ref[...] * dt2
    ll_re, ll_im = m2_ * jnp.cos(ang), m2_ * jnp.sin(ang)
    first = lax.broadcasted_iota(jnp.int32, ll_im.shape, 1) < p
    lam_ref[...] = jnp.concatenate(
        [ll_re, jnp.where(first, -ll_im, ll_im), ll_re, jnp.where(first, ll_im, -ll_im)]
        + [jnp.zeros_like(ll_re)] * 4, axis=0)


def _s5_prep(a_re, a_im, log_dt, b_re, b_im, c_re, c_im):
    g, p = a_re.shape
    w, l = S5_W, S5_L
    col = lambda x: x.reshape(g, p, 1)
    tile = lambda x: jnp.tile(x, (1, 1, l))
    dup = lambda x: jnp.concatenate([x, x], axis=-1).reshape(g, 1, 2 * p)
    ldt2 = jnp.broadcast_to(log_dt.reshape(g, 1, 1), (g, 1, 2 * p))
    args = (col(a_re), col(a_im), log_dt.reshape(g, 1, 1), tile(b_re), tile(b_im),
            tile(jnp.swapaxes(c_re, 1, 2)), tile(jnp.swapaxes(c_im, 1, 2)),
            dup(a_re), dup(a_im), ldt2)

    def gspec(shape):
        return pl.BlockSpec((None,) + shape, lambda i: (i,) + (0,) * len(shape))

    return pl.pallas_call(
        _s5_prep_kernel,
        out_shape=(jax.ShapeDtypeStruct((g, w, w), BF16),
                   jax.ShapeDtypeStruct((g, 2 * p, w), BF16),
                   jax.ShapeDtypeStruct((g, 2 * p, w), BF16),
                   jax.ShapeDtypeStruct((g, 2 * p, w), BF16),
                   jax.ShapeDtypeStruct((g, 8, 2 * p), F32)),
        grid=(g,),
        in_specs=[gspec((p, 1)), gspec((p, 1)), gspec((1, 1)), gspec((p, w)), gspec((p, w)),
                  gspec((p, w)), gspec((p, w)), gspec((1, 2 * p)), gspec((1, 2 * p)),
                  gspec((1, 2 * p))],
        out_specs=(gspec((w, w)), gspec((2 * p, w)), gspec((2 * p, w)), gspec((2 * p, w)),
                   gspec((8, 2 * p))),
        compiler_params=_params("parallel"),
        name="s5_prep",
    )(*args)


def _s5_inc_kernel(u_ref, m1_ref, m1s_ref, inc_ref, incs_ref):
    u = u_ref[...].astype(BF16)
    inc_ref[...] = _dot_nt(u, m1_ref[...])
    incs_ref[...] = _dot_nt(u, m1s_ref[...])


def _s5_inc(u_t, m1, m1s):
    g, n, w = u_t.shape
    sp = m1.shape[1]
    return pl.pallas_call(
        _s5_inc_kernel,
        out_shape=(jax.ShapeDtypeStruct((n, g * sp), F32), jax.ShapeDtypeStruct((n, g * sp), F32)),
        grid=(g,),
        in_specs=[pl.BlockSpec((None, n, w), lambda i: (i, 0, 0)),
                  pl.BlockSpec((None, sp, w), lambda i: (i, 0, 0)),
                  pl.BlockSpec((None, sp, w), lambda i: (i, 0, 0))],
        out_specs=(pl.BlockSpec((n, sp), lambda i: (0, i)), pl.BlockSpec((n, sp), lambda i: (0, i))),
        compiler_params=_params("parallel"),
        name="s5_inc",
    )(u_t, m1, m1s)


def _s5_scan_kernel(inc_ref, incs_ref, lam_ref, x_ref, *, bsz, n_steps):
    lam = lam_ref[...]
    a, bc, a_s, bc_s = lam[0:1, :], lam[1:2, :], lam[2:3, :], lam[3:4, :]
    width = inc_ref.shape[2]

    def step(n, carry):
        x, xs = carry
        x_ref[n] = x
        x_new = a * x + bc * xs + inc_ref[n]
        xs_new = a_s * xs + bc_s * x + incs_ref[n]
        return x_new, xs_new

    zero = jnp.zeros((bsz, width), F32)
    lax.fori_loop(0, n_steps, step, (zero, zero))


def _s5_scan(inc, incs, lam_rows, *, bsz, wb=2048):
    n, width = inc.shape
    n_steps = n // bsz
    blk = pl.BlockSpec((n_steps, bsz, wb), lambda i: (0, 0, i))
    x = pl.pallas_call(
        functools.partial(_s5_scan_kernel, bsz=bsz, n_steps=n_steps),
        out_shape=jax.ShapeDtypeStruct((n_steps, bsz, width), F32),
        grid=(width // wb,),
        in_specs=[blk, blk, pl.BlockSpec((8, wb), lambda i: (0, i))],
        out_specs=blk,
        compiler_params=_params("parallel"),
        name="s5_scan",
    )(inc.reshape(n_steps, bsz, width), incs.reshape(n_steps, bsz, width), lam_rows)
    return x.reshape(n, width)


def _s5_out_kernel(u_ref, x_ref, toep_ref, m2_ref, d_ref, o_ref):
    u = u_ref[...]
    y = _dot(u.astype(BF16), toep_ref[...]) + _dot(x_ref[...].astype(BF16), m2_ref[...])
    y = y + d_ref[...] * u
    c0 = math.sqrt(2.0 / math.pi)
    o_ref[...] = 0.5 * y * (1.0 + jnp.tanh(c0 * (y + 0.044715 * (y * y * y))))


def _s5_out(u_t, x_all, toep, m2, d_t):
    g, n, w = u_t.shape
    sp = m2.shape[1]
    return pl.pallas_call(
        _s5_out_kernel,
        out_shape=jax.ShapeDtypeStruct((g, n, w), F32),
        grid=(g,),
        in_specs=[pl.BlockSpec((None, n, w), lambda i: (i, 0, 0)),
                  pl.BlockSpec((n, sp), lambda i: (0, i)),
                  pl.BlockSpec((None, w, w), lambda i: (i, 0, 0)),
                  pl.BlockSpec((None, sp, w), lambda i: (i, 0, 0)),
                  pl.BlockSpec((None, 1, w), lambda i: (i, 0, 0))],
        out_specs=pl.BlockSpec((None, n, w), lambda i: (i, 0, 0)),
        compiler_params=_params("parallel"),
        name="s5_out",
    )(u_t, x_all, toep, m2, d_t)


def _s5_act(u, a_re, a_im, b_re, b_im, c_re, c_im, d_skip, log_dt, *, bsz, seq):
    t, d = u.shape
    g = d // S5_GROUP
    nc = seq // S5_L
    toep, m1, m1s, m2, lam = _s5_prep(a_re, a_im, log_dt, b_re, b_im, c_re, c_im)
    lam_rows = jnp.swapaxes(lam, 0, 1).reshape(8, g * 2 * S5_STATE)
    u_t = u.reshape(bsz, nc, S5_L, g, S5_GROUP).transpose(3, 1, 0, 2, 4).reshape(g, bsz * nc, S5_W)
    inc, incs = _s5_inc(u_t, m1, m1s)
    x_all = _s5_scan(inc, incs, lam_rows, bsz=bsz)
    d_t = jnp.tile(d_skip.reshape(g, 1, S5_GROUP), (1, 1, S5_L))
    act_t = _s5_out(u_t, x_all, toep, m2, d_t)
    return act_t.reshape(g, nc, bsz, S5_L, S5_GROUP).transpose(2, 1, 3, 0, 4).reshape(t, d)


def kernel(x, p, norm_mix, norm_mlp, norm_ple, w_in_e, w_out_e, hgrn_lb, g_norm_a, conv_w, a_log, dt_bias, g_norm_b, s5_a_re, s5_a_im, s5_b_re, s5_b_im, s5_c_re, s5_c_im, s5_d, s5_log_dt, w_glu, b_glu, w_out_o, w_up, w_down, w_ple_gate, w_ple_proj, final_norm):
    bsz, seq, d = x.shape
    t = bsz * seq
    depth = p.shape[0]
    heads, hd = N_HEADS, HEAD_DIM
    width = heads * hd
    main_cols = 8 * width
    lower_bounds = jnp.cumsum(jax.nn.softmax(hgrn_lb.astype(F32), axis=0), axis=0)
    row = lambda v: v.reshape(1, -1).astype(F32)

    h = x.reshape(t, d)
    p2 = p.reshape(depth, t, -1).astype(BF16)
    out = None
    for i in range(depth):
        j = i // 2
        if i % 2 == 0:
            w_in = w_in_e[j]
            w_small = jnp.pad(w_in[:, main_cols:], ((0, 0), (0, hd - 2 * heads))).astype(BF16)
            proj, small = _in_proj(h, row(norm_mix[i]), w_in[:, :main_cols].astype(BF16), w_small)
            proj = proj.reshape(bsz, seq, main_cols)
            small = small.reshape(bsz, seq, hd)
            o_a = _hgrn(proj, lower_bounds[i].reshape(heads, 1, hd), row(g_norm_a[j]),
                        bsz=bsz, seq=seq)
            cw = conv_w[j].reshape(CONV_WIDTH, 3, heads, hd).transpose(2, 1, 0, 3)
            scal = jnp.broadcast_to(jnp.stack([a_log[j], dt_bias[j]], axis=1)[:, :, None],
                                    (heads, 2, hd)).astype(F32)
            o_b = _delta(proj, small, cw, scal, row(g_norm_b[j]), bsz=bsz, seq=seq, col0=4 * heads)
            w_out = w_out_e[j].astype(BF16)
            h = _mm_resid([o_a.reshape(t, width), o_b.reshape(t, width)],
                          [w_out[:width], w_out[width:]], h)
        else:
            act = _s5_act(normed, s5_a_re[j], s5_a_im[j], s5_b_re[j], s5_b_im[j], s5_c_re[j], s5_c_im[j],
                          s5_d[j], s5_log_dt[j], bsz=bsz, seq=seq)
            glu = _glu(act, w_glu[j].astype(BF16), row(b_glu[j]))
            h = _mm_resid([glu], [w_out_o[j].astype(BF16)], h)
        h = _mlp(h, row(norm_mlp[i]), w_up[i].astype(BF16), w_down[i].astype(BF16))
        g_next = final_norm if i == depth - 1 else norm_mix[i + 1]
        h, normed = _ple(h, row(norm_ple[i]), w_ple_gate[i].astype(BF16), p2[i],
                         w_ple_proj[i].astype(BF16), row(g_next))
        out = normed
    return out.reshape(bsz, seq, d)
```

```python
import functools
import math

import jax
import jax.numpy as jnp
from jax import lax
from jax.experimental import pallas as pl
from jax.experimental.pallas import tpu as pltpu

F32 = jnp.float32
BF16 = jnp.bfloat16

NORM_EPS = 1e-6
CHUNK = 64
HEAD_DIM = 128
N_HEADS = 8
CONV_WIDTH = 4
S5_GROUP = 16
S5_STATE = 64
S5_L = 16
S5_W = S5_L * S5_GROUP
S5_PACK = 128 // S5_GROUP
EXP_CLAMP = 80.0

VMEM_LIMIT = 56 * 1024 * 1024


def _sigmoid(x):
    return 1.0 / (1.0 + jnp.exp(-x))


def _silu(x):
    return x * _sigmoid(x)


def _rms(x, g):
    return x * lax.rsqrt(jnp.mean(x * x, axis=-1, keepdims=True) + NORM_EPS) * g


def _dot(a, b):
    return jnp.dot(a, b, preferred_element_type=F32)


def _dot_nt(a, b):
    return lax.dot_general(a, b, (((1,), (1,)), ((), ())), preferred_element_type=F32)


def _dot_tn(a, b):
    return lax.dot_general(a, b, (((0,), (0,)), ((), ())), preferred_element_type=F32)


def _dot_split(a_bf, x):
    hi = x.astype(BF16)
    lo = (x - hi.astype(F32)).astype(BF16)
    return _dot(a_bf, hi) + _dot(a_bf, lo)


def _params(*sem):
    return pltpu.CompilerParams(dimension_semantics=sem, vmem_limit_bytes=VMEM_LIMIT)


def _in_proj_kernel(x_ref, g_ref, w_ref, ws_ref, o_ref, os_ref, hn_ref):
    @pl.when(pl.program_id(1) == 0)
    def _():
        hn = _rms(x_ref[...], g_ref[...]).astype(BF16)
        hn_ref[...] = hn
        os_ref[...] = _dot(hn, ws_ref[...])

    o_ref[...] = _dot(hn_ref[...], w_ref[...])


def _in_proj(x, g, w, w_small, *, tm=1024, tn=512):
    t, d = x.shape
    tm = min(tm, t)
    n = w.shape[1]
    ns = w_small.shape[1]
    return pl.pallas_call(
        _in_proj_kernel,
        out_shape=(jax.ShapeDtypeStruct((t, n), F32), jax.ShapeDtypeStruct((t, ns), F32)),
        grid=(t // tm, n // tn),
        in_specs=[pl.BlockSpec((tm, d), lambda i, j: (i, 0)),
                  pl.BlockSpec((1, d), lambda i, j: (0, 0)),
                  pl.BlockSpec((d, tn), lambda i, j: (0, j)),
                  pl.BlockSpec((d, ns), lambda i, j: (0, 0))],
        out_specs=(pl.BlockSpec((tm, tn), lambda i, j: (i, j)),
                   pl.BlockSpec((tm, ns), lambda i, j: (i, 0))),
        scratch_shapes=[pltpu.VMEM((tm, d), BF16)],
        compiler_params=_params("parallel", "arbitrary"),
        name="in_proj",
    )(x, g, w, w_small)


def _mm_resid_kernel(*refs, n_in):
    x_refs = refs[:n_in]
    w_refs = refs[n_in:2 * n_in]
    r_ref = refs[2 * n_in]
    o_ref = refs[2 * n_in + 1]
    acc = r_ref[...]
    for x_ref, w_ref in zip(x_refs, w_refs):
        acc = acc + _dot(x_ref[...].reshape(-1, x_ref.shape[-1]), w_ref[...])
    o_ref[...] = acc


def _mm_resid(xs, ws, resid, *, chunk_major_seq=None, tm=512, tn=512):
    t, n = resid.shape
    n_in = len(xs)
    if chunk_major_seq is None:
        x_specs = [pl.BlockSpec((tm, x.shape[1]), lambda i, j: (i, 0)) for x in xs]
    else:
        x_specs = [_chunk_major_spec(tm, x.shape[-1], chunk_major_seq) for x in xs]
    in_specs = (x_specs
                + [pl.BlockSpec((w.shape[0], tn), lambda i, j: (0, j)) for w in ws]
                + [pl.BlockSpec((tm, tn), lambda i, j: (i, j))])
    return pl.pallas_call(
        functools.partial(_mm_resid_kernel, n_in=n_in),
        out_shape=jax.ShapeDtypeStruct((t, n), F32),
        grid=(t // tm, n // tn),
        in_specs=in_specs,
        out_specs=pl.BlockSpec((tm, tn), lambda i, j: (i, j)),
        compiler_params=_params("parallel", "arbitrary"),
        name="mm_resid",
    )(*xs, *ws, resid)


def _mlp_kernel(h_ref, g_ref, wu_ref, wd_ref, o_ref, hn_ref):
    @pl.when(pl.program_id(1) == 0)
    def _():
        h = h_ref[...]
        hn_ref[...] = _rms(h, g_ref[...]).astype(BF16)
        o_ref[...] = h

    a = jnp.maximum(_dot(hn_ref[...], wu_ref[...]), 0.0)
    o_ref[...] += _dot((a * a).astype(BF16), wd_ref[...])


def _mlp(h, g, w_up, w_down, *, tm=512, tf=512):
    t, d = h.shape
    f = w_up.shape[1]
    return pl.pallas_call(
        _mlp_kernel,
        out_shape=jax.ShapeDtypeStruct((t, d), F32),
        grid=(t // tm, f // tf),
        in_specs=[pl.BlockSpec((tm, d), lambda i, j: (i, 0)),
                  pl.BlockSpec((1, d), lambda i, j: (0, 0)),
                  pl.BlockSpec((d, tf), lambda i, j: (0, j)),
                  pl.BlockSpec((tf, d), lambda i, j: (j, 0))],
        out_specs=pl.BlockSpec((tm, d), lambda i, j: (i, 0)),
        scratch_shapes=[pltpu.VMEM((tm, d), BF16)],
        compiler_params=_params("parallel", "arbitrary"),
        name="mlp",
    )(h, g, w_up, w_down)


def _ple_kernel(h_ref, g_ref, wg_ref, p_ref, wp_ref, g2_ref, o_ref, on_ref, hn_ref, *, tn):
    j = pl.program_id(1)

    @pl.when(j == 0)
    def _():
        hn_ref[...] = _rms(h_ref[...], g_ref[...]).astype(BF16)

    col = pl.multiple_of(j * tn, tn)
    gate = _sigmoid(_dot(hn_ref[...], wg_ref[...]))
    emb = _dot(p_ref[...], wp_ref[...])
    o_ref[:, pl.ds(col, tn)] = h_ref[:, pl.ds(col, tn)] + gate * emb

    @pl.when(j == pl.num_programs(1) - 1)
    def _():
        on_ref[...] = _rms(o_ref[...], g2_ref[...]).reshape(on_ref.shape)


def _chunk_major_spec(tm, width, seq):
    per_seq = seq // tm
    return pl.BlockSpec((tm // S5_L, None, S5_L, width),
                        lambda i, j: (i % per_seq, i // per_seq, 0, 0))


def _ple(h, g, w_gate, p, w_proj, g_next, *, chunk_major=None, tm=512, tn=512):
    t, d = h.shape
    pd = p.shape[1]
    if chunk_major is None:
        normed_shape, normed_spec = (t, d), pl.BlockSpec((tm, d), lambda i, j: (i, 0))
    else:
        bsz, seq = chunk_major
        normed_shape, normed_spec = (seq // S5_L, bsz, S5_L, d), _chunk_major_spec(tm, d, seq)
    return pl.pallas_call(
        functools.partial(_ple_kernel, tn=tn),
        out_shape=(jax.ShapeDtypeStruct((t, d), F32), jax.ShapeDtypeStruct(normed_shape, F32)),
        grid=(t // tm, d // tn),
        in_specs=[pl.BlockSpec((tm, d), lambda i, j: (i, 0)),
                  pl.BlockSpec((1, d), lambda i, j: (0, 0)),
                  pl.BlockSpec((d, tn), lambda i, j: (0, j)),
                  pl.BlockSpec((tm, pd), lambda i, j: (i, 0)),
                  pl.BlockSpec((pd, tn), lambda i, j: (0, j)),
                  pl.BlockSpec((1, d), lambda i, j: (0, 0))],
        out_specs=(pl.BlockSpec((tm, d), lambda i, j: (i, 0)), normed_spec),
        scratch_shapes=[pltpu.VMEM((tm, d), BF16)],
        compiler_params=_params("parallel", "arbitrary"),
        name="ple",
    )(h, g, w_gate, p, w_proj, g_next)


def _glu_kernel(a_ref, w_ref, b_ref, o_ref, abf_ref, *, tn):
    j = pl.program_id(1)

    @pl.when(j == 0)
    def _():
        abf_ref[...] = a_ref[...].astype(BF16)

    col = pl.multiple_of(j * tn, tn)
    z = _dot(abf_ref[...], w_ref[...]) + b_ref[...]
    o_ref[...] = (a_ref[:, pl.ds(col, tn)] * _sigmoid(z)).astype(BF16)


def _glu(act, w, b, *, tm=512, tn=512):
    t, d = act.shape
    return pl.pallas_call(
        functools.partial(_glu_kernel, tn=tn),
        out_shape=jax.ShapeDtypeStruct((t, d), BF16),
        grid=(t // tm, d // tn),
        in_specs=[pl.BlockSpec((tm, d), lambda i, j: (i, 0)),
                  pl.BlockSpec((d, tn), lambda i, j: (0, j)),
                  pl.BlockSpec((1, tn), lambda i, j: (0, j))],
        out_specs=pl.BlockSpec((tm, tn), lambda i, j: (i, j)),
        scratch_shapes=[pltpu.VMEM((tm, d), BF16)],
        compiler_params=_params("parallel", "arbitrary"),
        name="glu",
    )(act, w, b)


def _row(x, t):
    return x[t:t + 1, :]


def _run_lockstep(gens):
    live = list(gens)
    while live:
        nxt = []
        for g in live:
            try:
                next(g)
                nxt.append(g)
            except StopIteration:
                pass
        live = nxt


def _hgrn_kernel(q_ref, f_ref, i_ref, g_ref, lb_ref, gn_ref, o_ref, st_ref, *, n_chunks, hb):
    c_len, d = CHUNK, HEAD_DIM

    @pl.when(pl.program_id(2) == 0)
    def _():
        st_ref[...] = jnp.zeros_like(st_ref)

    ri = lax.broadcasted_iota(jnp.int32, (c_len, d), 0)
    rt = lax.broadcasted_iota(jnp.int32, (c_len, c_len), 0)
    rs = lax.broadcasted_iota(jnp.int32, (c_len, c_len), 1)
    tril = (rs <= rt).astype(BF16)
    diag_mask = jnp.logical_and(rs <= rt, (rs // 16) == (rt // 16))
    gn = gn_ref[...]

    def one_head(hh, r0):
        rows = pl.ds(r0, c_len)
        cols = slice(hh * d, (hh + 1) * d)
        lb = lb_ref[hh]
        q = q_ref[rows, cols]
        forget = lb + (1.0 - lb) * _sigmoid(f_ref[rows, cols])
        k = 1.0 - forget
        v = i_ref[rows, cols].astype(BF16)
        cum = _dot_split(tril, jnp.log(forget))
        yield

        def side(valid, ref_row, sign, x):
            e = jnp.where(valid, sign * (cum - ref_row), 0.0)
            return jnp.where(valid, x * jnp.exp(e), 0.0)

        c31 = _row(cum, 31)
        ref_b = jnp.where(ri < 32, _row(cum, 15), _row(cum, 47))
        ref_d = jnp.where(ri < 16, _row(cum, 8),
                          jnp.where(ri < 32, _row(cum, 24),
                                    jnp.where(ri < 48, _row(cum, 40), _row(cum, 56))))
        hi16 = (ri % 32) >= 16
        q_b = side(hi16, ref_b, 1.0, q)
        k_b = side(jnp.logical_not(hi16), ref_b, -1.0, k)
        q_off = jnp.concatenate([side(ri >= 32, c31, 1.0, q),
                                 jnp.where(ri < 32, q_b, 0.0),
                                 jnp.where(ri >= 32, q_b, 0.0)], axis=1).astype(BF16)
        k_off = jnp.concatenate([side(ri < 32, c31, -1.0, k),
                                 jnp.where(ri < 32, k_b, 0.0),
                                 jnp.where(ri >= 32, k_b, 0.0)], axis=1).astype(BF16)
        q_d = (q * jnp.exp(jnp.minimum(cum - ref_d, EXP_CLAMP))).astype(BF16)
        k_d = (k * jnp.exp(jnp.minimum(ref_d - cum, EXP_CLAMP))).astype(BF16)
        s_off = _dot_nt(q_off, k_off)
        s_diag = _dot_nt(q_d, k_d)
        cum_end = _row(cum, c_len - 1)
        q_dec = (q * jnp.exp(cum)).astype(BF16)
        k_dec = (k * jnp.exp(cum_end - cum)).astype(BF16)
        st = st_ref[hh]
        out_st = _dot_nt(q_dec, st.astype(BF16))
        st_ref[hh] = st * jnp.exp(cum_end) + _dot_tn(v, k_dec)
        yield
        scores = s_off + jnp.where(diag_mask, s_diag, 0.0)
        out = _dot(scores.astype(BF16), v) + out_st
        yield

        out = _rms(out, gn) * _silu(g_ref[rows, cols])
        o_ref[rows, cols] = out.astype(o_ref.dtype)

    def chunk(c, carry):
        r0 = pl.multiple_of(c * c_len, c_len)
        _run_lockstep([one_head(hh, r0) for hh in range(hb)])
        return carry

    lax.fori_loop(0, n_chunks, chunk, 0)


def _hgrn(proj, lb, g_norm, *, bsz, seq, sb=512, hb=8):
    sb = min(sb, seq)
    h = N_HEADS
    wb = hb * HEAD_DIM

    def spec(off):
        return pl.BlockSpec((None, sb, wb), lambda b, hh, s: (b, s, hh + off // hb))

    return pl.pallas_call(
        functools.partial(_hgrn_kernel, n_chunks=sb // CHUNK, hb=hb),
        out_shape=jax.ShapeDtypeStruct((bsz, seq, h * HEAD_DIM), BF16),
        grid=(bsz, h // hb, seq // sb),
        in_specs=[spec(0), spec(h), spec(2 * h), spec(3 * h),
                  pl.BlockSpec((hb, 1, HEAD_DIM), lambda b, hh, s: (hh, 0, 0)),
                  pl.BlockSpec((1, HEAD_DIM), lambda b, hh, s: (0, 0))],
        out_specs=pl.BlockSpec((None, sb, wb), lambda b, hh, s: (b, s, hh)),
        scratch_shapes=[pltpu.VMEM((hb, HEAD_DIM, HEAD_DIM), F32)],
        compiler_params=_params("parallel", "parallel", "arbitrary"),
        name="hgrn2",
    )(proj, proj, proj, proj, lb, g_norm)


def _delta_kernel(q_ref, k_ref, v_ref, z_ref, ab_ref, cw_ref, sc_ref, gn_ref, o_ref,
                  st_ref, hist_ref, *, n_chunks, hb):
    c_len, d = CHUNK, HEAD_DIM
    head0 = pl.program_id(1) * hb

    @pl.when(pl.program_id(2) == 0)
    def _():
        st_ref[...] = jnp.zeros_like(st_ref)
        hist_ref[...] = jnp.zeros_like(hist_ref)

    rt = lax.broadcasted_iota(jnp.int32, (c_len, c_len), 0)
    rs = lax.broadcasted_iota(jnp.int32, (c_len, c_len), 1)
    causal = rs <= rt
    strict = rs < rt
    tril = causal.astype(BF16)
    ones_cc = jnp.ones((c_len, c_len), BF16)
    eye = (rs == rt).astype(F32)
    bt, bs = rt // 16, rs // 16
    m_diag = jnp.logical_and(strict, bt == bs)
    m_l1 = jnp.logical_and(bt // 2 == bs // 2, bt == bs + 1)
    m_l2 = jnp.logical_and(bt >= 2, bs < 2)
    lane = lax.broadcasted_iota(jnp.int32, (c_len, d), 1)
    gn = gn_ref[...]

    def conv(ref, which, hh, c, r0):
        cols = slice(hh * d, (hh + 1) * d)
        cw = cw_ref[hh, which]
        prev0 = pl.multiple_of(jnp.maximum(r0 - 8, 0), 8)
        prev = jnp.where(c > 0, ref[pl.ds(prev0, 8), cols], hist_ref[which, :, cols])
        win = jnp.concatenate([prev, ref[pl.ds(r0, c_len), cols]], axis=0)
        acc = win * cw[CONV_WIDTH - 1:CONV_WIDTH, :]
        for j in range(1, CONV_WIDTH):
            shifted = pltpu.roll(win, j, axis=0)
            acc = acc + shifted * cw[CONV_WIDTH - 1 - j:CONV_WIDTH - j, :]
        return _silu(acc[8:, :])

    def one_head(hh, c, r0):
        rows = pl.ds(r0, c_len)
        cols = slice(hh * d, (hh + 1) * d)
        head = head0 + hh
        neg_a = -jnp.exp(sc_ref[hh, 0:1, :])
        dt_bias = sc_ref[hh, 1:2, :]
        q = conv(q_ref, 0, hh, c, r0)
        k = conv(k_ref, 1, hh, c, r0)
        v = conv(v_ref, 2, hh, c, r0)
        q = q * lax.rsqrt(jnp.sum(q * q, axis=-1, keepdims=True) + NORM_EPS) * (d ** -0.5)
        k = k * lax.rsqrt(jnp.sum(k * k, axis=-1, keepdims=True) + NORM_EPS)
        yield

        ab = ab_ref[rows, :]
        a_col = jnp.sum(jnp.where(lane == head, ab, 0.0), axis=-1, keepdims=True)
        b_col = jnp.sum(jnp.where(lane == head + N_HEADS, ab, 0.0), axis=-1, keepdims=True)
        beta = _sigmoid(b_col)
        x = a_col + dt_bias
        log_a = neg_a * (jnp.maximum(x, 0.0) + jnp.log(1.0 + jnp.exp(-jnp.abs(x))))
        cum = _dot_split(tril, log_a)
        cum_row = _dot_split(ones_cc, jnp.where(rs >= rt, log_a[:, :c_len], 0.0))
        k_bf = k.astype(BF16)
        k_beta = k * beta
        kk = _dot_nt(k_beta.astype(BF16), k_bf)
        qk = _dot_nt(q.astype(BF16), k_bf)
        yield
        decay = jnp.exp(jnp.where(causal, cum[:, :c_len] - cum_row, 0.0))
        n_mat = jnp.where(strict, kk * decay, 0.0)

        def mm(a, b):
            return _dot(a.astype(BF16), b.astype(BF16))

        n_d = jnp.where(m_diag, n_mat, 0.0)
        p2 = mm(n_d, n_d)
        yield
        p4 = mm(p2, p2)
        a12 = mm(eye - n_d, eye + p2)
        yield
        p8 = mm(p4, p4)
        yield
        a48 = mm(eye + p4, eye + p8)
        yield
        t_d = mm(a12, a48)
        yield
        x1 = mm(t_d, jnp.where(m_l1, n_mat, 0.0))
        yield
        t_32 = t_d - mm(x1, t_d)
        yield
        x2 = mm(t_32, jnp.where(m_l2, n_mat, 0.0))
        yield
        t_inv = t_32 - mm(x2, t_32)
        yield

        e_cum = jnp.exp(cum)
        rhs = jnp.concatenate([v * beta, k_beta * e_cum], axis=1)
        sol = mm(t_inv, rhs)
        yield
        u, w = sol[:, :d], sol[:, d:]
        intra = jnp.where(causal, qk * decay, 0.0)
        cum_end = _row(cum, c_len - 1)
        q_dec = (q * e_cum).astype(BF16)
        k_dec = (k * jnp.exp(cum_end - cum)).astype(BF16)

        st = st_ref[hh]
        st_bf = st.astype(BF16)
        v_new = u - _dot(w.astype(BF16), st_bf)
        out_st = _dot(q_dec, st_bf)
        yield
        v_new_bf = v_new.astype(BF16)
        out = out_st + _dot(intra.astype(BF16), v_new_bf)
        st_ref[hh] = st * jnp.exp(cum_end) + _dot_tn(k_dec, v_new_bf)
        yield

        out = _rms(out, gn) * _silu(z_ref[rows, cols])
        o_ref[rows, cols] = out.astype(o_ref.dtype)

    def chunk(c, carry):
        r0 = pl.multiple_of(c * c_len, c_len)
        _run_lockstep([one_head(hh, c, r0) for hh in range(hb)])
        return carry

    lax.fori_loop(0, n_chunks, chunk, 0)
    last8 = pl.ds(n_chunks * c_len - 8, 8)
    hist_ref[0] = q_ref[last8, :]
    hist_ref[1] = k_ref[last8, :]
    hist_ref[2] = v_ref[last8, :]


def _delta(proj, small, conv_w, scal, g_norm, *, bsz, seq, col0, sb=512, hb=8):
    sb = min(sb, seq)
    h = N_HEADS
    wb = hb * HEAD_DIM

    def spec(off):
        return pl.BlockSpec((None, sb, wb), lambda b, hh, s: (b, s, hh + off // hb))

    return pl.pallas_call(
        functools.partial(_delta_kernel, n_chunks=sb // CHUNK, hb=hb),
        out_shape=jax.ShapeDtypeStruct((bsz, seq, h * HEAD_DIM), BF16),
        grid=(bsz, h // hb, seq // sb),
        in_specs=[spec(col0), spec(col0 + h), spec(col0 + 2 * h), spec(col0 + 3 * h),
                  pl.BlockSpec((None, sb, HEAD_DIM), lambda b, hh, s: (b, s, 0)),
                  pl.BlockSpec((hb, 3, CONV_WIDTH, HEAD_DIM), lambda b, hh, s: (hh, 0, 0, 0)),
                  pl.BlockSpec((hb, 2, HEAD_DIM), lambda b, hh, s: (hh, 0, 0)),
                  pl.BlockSpec((1, HEAD_DIM), lambda b, hh, s: (0, 0))],
        out_specs=pl.BlockSpec((None, sb, wb), lambda b, hh, s: (b, s, hh)),
        scratch_shapes=[pltpu.VMEM((hb, HEAD_DIM, HEAD_DIM), F32),
                        pltpu.VMEM((3, 8, wb), F32)],
        compiler_params=_params("parallel", "parallel", "arbitrary"),
        name="deltanet",
    )(proj, proj, proj, proj, small, conv_w, scal, g_norm)


def _cmul(ar, ai, br, bi):
    return ar * br - ai * bi, ar * bi + ai * br


def _s5_expand(a, g8):
    gs, lanes = S5_GROUP, S5_PACK * S5_GROUP
    sel = (lax.broadcasted_iota(jnp.int32, (a.shape[0], lanes), 1) // gs) == g8
    pieces = []
    for k in range(S5_L):
        src = a[:, (k // S5_PACK) * lanes:(k // S5_PACK + 1) * lanes]
        shift = ((g8 - k % S5_PACK + S5_PACK) * gs) % lanes
        pieces.append(jnp.where(sel, pltpu.roll(src, shift, axis=1), 0.0))
    return jnp.concatenate(pieces, axis=1)


def _s5_prep_kernel(are_ref, aim_ref, ldt_ref, bre_ref, bim_ref, cre_ref, cim_ref,
                    are2_ref, aim2_ref, ldt2_ref, w8_ref, m1_ref, m2_ref, lam_ref):
    def one_group(g8, carry):
        _s5_prep_group(g8, are_ref[g8], aim_ref[g8], ldt_ref[g8], bre_ref[g8], bim_ref[g8],
                       cre_ref[g8], cim_ref[g8], are2_ref[g8], aim2_ref[g8], ldt2_ref[g8],
                       w8_ref, m1_ref, m2_ref, lam_ref)
        return carry

    lax.fori_loop(0, S5_PACK, one_group, 0)


def _s5_prep_group(g8, are, aim, ldt, bre, bim, cre, cim, are2, aim2, ldt2,
                   w8_ref, m1_ref, m2_ref, lam_ref):
    p, w, l, gs = S5_STATE, S5_W, S5_L, S5_GROUP
    lanes = S5_PACK * gs
    hp = lax.Precision.HIGHEST
    dt = jnp.exp(ldt)
    a_re = jnp.broadcast_to(are, (p, w))
    a_im = jnp.broadcast_to(aim, (p, w))
    lr, li = a_re * dt, a_im * dt
    mag = jnp.exp(lr)
    lb_re, lb_im = mag * jnp.cos(li), mag * jnp.sin(li)
    den = a_re * a_re + a_im * a_im
    xr, xi = lb_re - 1.0, lb_im
    coef_re, coef_im = (xr * a_re + xi * a_im) / den, (xi * a_re - xr * a_im) / den
    bb_re, bb_im = _cmul(coef_re, coef_im, bre, bim)

    kf = (lax.broadcasted_iota(jnp.int32, (p, w), 1) // gs).astype(F32)

    def lam_pow(e):
        m = jnp.exp(e * lr)
        return m * jnp.cos(e * li), m * jnp.sin(e * li)

    e_re, e_im = _cmul(*lam_pow(kf), cre, cim)
    lhs = jnp.concatenate([bb_re[:, :gs], -bb_im[:, :gs]], axis=0)
    rhs = jnp.concatenate([e_re, e_im], axis=0)
    r0 = lax.dot_general(lhs, rhs, (((0,), (0,)), ((), ())), precision=hp,
                         preferred_element_type=F32)
    r0x = _s5_expand(r0, g8).astype(w8_ref.dtype)
    for s in range(l):
        rows = pl.ds(pl.multiple_of(s * lanes + g8 * gs, gs), gs)
        if s:
            w8_ref[rows, :s * lanes] = jnp.zeros((gs, s * lanes), w8_ref.dtype)
        w8_ref[rows, s * lanes:] = r0x[:, :(l - s) * lanes]

    st_rows = pl.ds(pl.multiple_of(g8 * 2 * p, 2 * p), 2 * p)
    d_re, d_im = _cmul(*lam_pow(float(l - 1) - kf), bb_re, bb_im)
    m1_ref[st_rows, :] = _s5_expand(jnp.concatenate([d_re, d_im], axis=0), g8).astype(m1_ref.dtype)
    f_re, f_im = _cmul(*lam_pow(kf + 1.0), cre, cim)
    m2_ref[st_rows, :] = _s5_expand(jnp.concatenate([f_re, -f_im], axis=0), g8).astype(m2_ref.dtype)

    dt2 = jnp.exp(ldt2)
    mag_l = jnp.exp(float(l) * are2 * dt2)
    ang = float(l) * aim2 * dt2
    ll_re, ll_im = mag_l * jnp.cos(ang), mag_l * jnp.sin(ang)
    first = lax.broadcasted_iota(jnp.int32, ll_im.shape, 1) < p
    lam_ref[g8] = jnp.concatenate([ll_re, jnp.where(first, -ll_im, ll_im)]
                                  + [jnp.zeros_like(ll_re)] * 6, axis=0)


def _s5_prep(a_re, a_im, log_dt, b_re, b_im, c_re, c_im):
    g, p = a_re.shape
    w, l, pk = S5_W, S5_L, S5_PACK
    wx = l * pk * S5_GROUP
    col = lambda x: x.reshape(g, p, 1)
    tile = lambda x: jnp.tile(x, (1, 1, l))
    dup = lambda x: jnp.concatenate([x, x], axis=-1).reshape(g, 1, 2 * p)
    ldt2 = jnp.broadcast_to(log_dt.reshape(g, 1, 1), (g, 1, 2 * p))
    args = (col(a_re), col(a_im), log_dt.reshape(g, 1, 1), tile(b_re), tile(b_im),
            tile(jnp.swapaxes(c_re, 1, 2)), tile(jnp.swapaxes(c_im, 1, 2)),
            dup(a_re), dup(a_im), ldt2)

    def gspec(shape):
        return pl.BlockSpec((pk,) + shape, lambda i: (i,) + (0,) * len(shape))

    def ospec(shape):
        return pl.BlockSpec((None,) + shape, lambda i: (i,) + (0,) * len(shape))

    return pl.pallas_call(
        _s5_prep_kernel,
        out_shape=(jax.ShapeDtypeStruct((g // pk, wx, wx), BF16),
                   jax.ShapeDtypeStruct((g // pk, pk * 2 * p, wx), BF16),
                   jax.ShapeDtypeStruct((g // pk, pk * 2 * p, wx), BF16),
                   jax.ShapeDtypeStruct((g, 8, 2 * p), F32)),
        grid=(g // pk,),
        in_specs=[gspec((p, 1)), gspec((p, 1)), gspec((1, 1)), gspec((p, w)), gspec((p, w)),
                  gspec((p, w)), gspec((p, w)), gspec((1, 2 * p)), gspec((1, 2 * p)),
                  gspec((1, 2 * p))],
        out_specs=(ospec((wx, wx)), ospec((pk * 2 * p, wx)), ospec((pk * 2 * p, wx)),
                   gspec((8, 2 * p))),
        compiler_params=_params("parallel"),
        name="s5_prep",
    )(*args)


def _s5_gather(u_ref, x8_ref, nbk):
    lanes = u_ref.shape[1]
    for t in range(S5_L):
        x8_ref[:, t * lanes:(t + 1) * lanes] = u_ref[pl.ds(t, nbk, stride=S5_L), :].astype(BF16)


def _s5_inc_kernel(u_ref, m1_ref, inc_ref, x8_ref):
    _s5_gather(u_ref, x8_ref, inc_ref.shape[0])
    inc_ref[...] = _dot_nt(x8_ref[...], m1_ref[...])


def _s5_inc(u, m1, *, rb=4):
    t, d = u.shape
    ngb, sp, wx = m1.shape
    lanes = d // ngb
    rows = t // rb
    nbk = rows // S5_L
    return pl.pallas_call(
        _s5_inc_kernel,
        out_shape=jax.ShapeDtypeStruct((t // S5_L, ngb * sp), F32),
        grid=(ngb, rb),
        in_specs=[pl.BlockSpec((rows, lanes), lambda i, r: (r, i)),
                  pl.BlockSpec((None, sp, wx), lambda i, r: (i, 0, 0))],
        out_specs=pl.BlockSpec((nbk, sp), lambda i, r: (r, i)),
        scratch_shapes=[pltpu.VMEM((nbk, wx), BF16)],
        compiler_params=_params("parallel", "arbitrary"),
        name="s5_inc",
    )(u, m1)


def _s5_scan_kernel(inc_ref, lam_ref, x_ref, *, bsz, n_steps):
    lam = lam_ref[...]
    a, bc = lam[0:1, :], lam[1:2, :]
    width = inc_ref.shape[2]
    sp = 2 * S5_STATE
    first = (lax.broadcasted_iota(jnp.int32, (bsz, width), 1) % sp) < S5_STATE

    def swap(v):
        return jnp.where(first, pltpu.roll(v, width - S5_STATE, axis=1),
                         pltpu.roll(v, S5_STATE, axis=1))

    def step(n, carry):
        x, xs = carry
        x_ref[n] = x
        inc = inc_ref[n]
        return a * x + bc * xs + inc, a * xs - bc * x + swap(inc)

    zero = jnp.zeros((bsz, width), F32)
    lax.fori_loop(0, n_steps, step, (zero, zero))


def _s5_scan(inc, lam_rows, *, bsz, wb=2048):
    n, width = inc.shape
    n_steps = n // bsz
    blk = pl.BlockSpec((n_steps, bsz, wb), lambda i: (0, 0, i))
    x = pl.pallas_call(
        functools.partial(_s5_scan_kernel, bsz=bsz, n_steps=n_steps),
        out_shape=jax.ShapeDtypeStruct((n_steps, bsz, width), F32),
        grid=(width // wb,),
        in_specs=[blk, pl.BlockSpec((8, wb), lambda i: (0, i))],
        out_specs=blk,
        compiler_params=_params("parallel"),
        name="s5_scan",
    )(inc.reshape(n_steps, bsz, width), lam_rows)
    return x.reshape(n, width)


def _s5_out_kernel(u_ref, x_ref, w8_ref, m2_ref, d_ref, o_ref, x8_ref):
    nbk = x_ref.shape[0]
    lanes = u_ref.shape[1]
    _s5_gather(u_ref, x8_ref, nbk)
    y8 = _dot(x8_ref[...], w8_ref[...]) + _dot(x_ref[...].astype(BF16), m2_ref[...])
    c0 = math.sqrt(2.0 / math.pi)
    d_skip = d_ref[...]
    for t in range(S5_L):
        rows = pl.ds(t, nbk, stride=S5_L)
        y = y8[:, t * lanes:(t + 1) * lanes] + d_skip * u_ref[rows, :]
        o_ref[rows, :] = 0.5 * y * (1.0 + jnp.tanh(c0 * (y + 0.044715 * (y * y * y))))


def _s5_out(u, x_all, w8, m2, d_skip, *, rb=4):
    t, d = u.shape
    ngb, sp, wx = m2.shape
    lanes = d // ngb
    rows = t // rb
    nbk = rows // S5_L
    return pl.pallas_call(
        _s5_out_kernel,
        out_shape=jax.ShapeDtypeStruct((t, d), F32),
        grid=(ngb, rb),
        in_specs=[pl.BlockSpec((rows, lanes), lambda i, r: (r, i)),
                  pl.BlockSpec((nbk, sp), lambda i, r: (r, i)),
                  pl.BlockSpec((None, wx, wx), lambda i, r: (i, 0, 0)),
                  pl.BlockSpec((None, sp, wx), lambda i, r: (i, 0, 0)),
                  pl.BlockSpec((1, lanes), lambda i, r: (0, i))],
        out_specs=pl.BlockSpec((rows, lanes), lambda i, r: (r, i)),
        scratch_shapes=[pltpu.VMEM((nbk, wx), BF16)],
        compiler_params=_params("parallel", "arbitrary"),
        name="s5_out",
    )(u, x_all, w8, m2, d_skip)


def _s5_act(u, a_re, a_im, b_re, b_im, c_re, c_im, d_skip, log_dt, *, bsz):
    d = u.shape[1]
    g = d // S5_GROUP
    w8, m1, m2, lam = _s5_prep(a_re, a_im, log_dt, b_re, b_im, c_re, c_im)
    lam_rows = jnp.swapaxes(lam, 0, 1).reshape(8, g * 2 * S5_STATE)
    inc = _s5_inc(u, m1)
    x_all = _s5_scan(inc, lam_rows, bsz=bsz)
    return _s5_out(u, x_all, w8, m2, d_skip.reshape(1, d).astype(F32))


def kernel(x, p, norm_mix, norm_mlp, norm_ple, w_in_e, w_out_e, hgrn_lb, g_norm_a, conv_w, a_log, dt_bias, g_norm_b, s5_a_re, s5_a_im, s5_b_re, s5_b_im, s5_c_re, s5_c_im, s5_d, s5_log_dt, w_glu, b_glu, w_out_o, w_up, w_down, w_ple_gate, w_ple_proj, final_norm):
    bsz, seq, d = x.shape
    t = bsz * seq
    depth = p.shape[0]
    heads, hd = N_HEADS, HEAD_DIM
    width = heads * hd
    main_cols = 8 * width
    lower_bounds = jnp.cumsum(jax.nn.softmax(hgrn_lb.astype(F32), axis=0), axis=0)
    row = lambda v: v.reshape(1, -1).astype(F32)

    h = x.reshape(t, d)
    p2 = p.reshape(depth, t, -1).astype(BF16)
    out = None
    for i in range(depth):
        j = i // 2
        if i % 2 == 0:
            w_in = w_in_e[j]
            w_small = jnp.pad(w_in[:, main_cols:], ((0, 0), (0, hd - 2 * heads))).astype(BF16)
            proj, small = _in_proj(h, row(norm_mix[i]), w_in[:, :main_cols].astype(BF16), w_small)
            proj = proj.reshape(bsz, seq, main_cols)
            small = small.reshape(bsz, seq, hd)
            o_a = _hgrn(proj, lower_bounds[i].reshape(heads, 1, hd), row(g_norm_a[j]),
                        bsz=bsz, seq=seq)
            cw = conv_w[j].reshape(CONV_WIDTH, 3, heads, hd).transpose(2, 1, 0, 3)
            scal = jnp.broadcast_to(jnp.stack([a_log[j], dt_bias[j]], axis=1)[:, :, None],
                                    (heads, 2, hd)).astype(F32)
            o_b = _delta(proj, small, cw, scal, row(g_norm_b[j]), bsz=bsz, seq=seq, col0=4 * heads)
            w_out = w_out_e[j].astype(BF16)
            h = _mm_resid([o_a.reshape(t, width), o_b.reshape(t, width)],
                          [w_out[:width], w_out[width:]], h)
        else:
            act = _s5_act(normed.reshape(t, d), s5_a_re[j], s5_a_im[j], s5_b_re[j], s5_b_im[j],
                          s5_c_re[j], s5_c_im[j], s5_d[j], s5_log_dt[j], bsz=bsz)
            glu = _glu(act, w_glu[j].astype(BF16), row(b_glu[j]))
            h = _mm_resid([glu.reshape(seq // S5_L, bsz, S5_L, d)], [w_out_o[j].astype(BF16)], h,
                          chunk_major_seq=seq)
        h = _mlp(h, row(norm_mlp[i]), w_up[i].astype(BF16), w_down[i].astype(BF16))
        last = i == depth - 1
        feeds_s5 = not last and (i + 1) % 2 == 1
        h, normed = _ple(h, row(norm_ple[i]), w_ple_gate[i].astype(BF16), p2[i],
                         w_ple_proj[i].astype(BF16), row(final_norm if last else norm_mix[i + 1]),
                         chunk_major=(bsz, seq) if feeds_s5 else None)
        out = normed
    return out.reshape(bsz, seq, d)
```

```python
import functools
import math

import jax
import jax.numpy as jnp
from jax import lax
from jax.experimental import pallas as pl
from jax.experimental.pallas import tpu as pltpu

F32 = jnp.float32
BF16 = jnp.bfloat16

NORM_EPS = 1e-6
CHUNK = 64
HEAD_DIM = 128
N_HEADS = 8
CONV_WIDTH = 4
S5_GROUP = 16
S5_STATE = 64
S5_L = 16
S5_W = S5_L * S5_GROUP
S5_PACK = 128 // S5_GROUP
EXP_CLAMP = 80.0

VMEM_LIMIT = 56 * 1024 * 1024


def _sigmoid(x):
    return 1.0 / (1.0 + jnp.exp(-x))


def _silu(x):
    return x * _sigmoid(x)


def _rms(x, g):
    return x * lax.rsqrt(jnp.mean(x * x, axis=-1, keepdims=True) + NORM_EPS) * g


def _dot(a, b):
    return jnp.dot(a, b, preferred_element_type=F32)


def _dot_nt(a, b):
    return lax.dot_general(a, b, (((1,), (1,)), ((), ())), preferred_element_type=F32)


def _dot_tn(a, b):
    return lax.dot_general(a, b, (((0,), (0,)), ((), ())), preferred_element_type=F32)


def _dot_split(a_bf, x):
    hi = x.astype(BF16)
    lo = (x - hi.astype(F32)).astype(BF16)
    return _dot(a_bf, hi) + _dot(a_bf, lo)


def _params(*sem):
    return pltpu.CompilerParams(dimension_semantics=sem, vmem_limit_bytes=VMEM_LIMIT)


def _in_proj_kernel(x_ref, g_ref, w_ref, ws_ref, o_ref, os_ref, hn_ref):
    @pl.when(pl.program_id(1) == 0)
    def _():
        hn = _rms(x_ref[...], g_ref[...]).astype(BF16)
        hn_ref[...] = hn
        os_ref[...] = _dot(hn, ws_ref[...])

    o_ref[...] = _dot(hn_ref[...], w_ref[...])


def _in_proj(x, g, w, w_small, *, tm=1024, tn=512):
    t, d = x.shape
    tm = min(tm, t)
    n = w.shape[1]
    ns = w_small.shape[1]
    return pl.pallas_call(
        _in_proj_kernel,
        out_shape=(jax.ShapeDtypeStruct((t, n), F32), jax.ShapeDtypeStruct((t, ns), F32)),
        grid=(t // tm, n // tn),
        in_specs=[pl.BlockSpec((tm, d), lambda i, j: (i, 0)),
                  pl.BlockSpec((1, d), lambda i, j: (0, 0)),
                  pl.BlockSpec((d, tn), lambda i, j: (0, j)),
                  pl.BlockSpec((d, ns), lambda i, j: (0, 0))],
        out_specs=(pl.BlockSpec((tm, tn), lambda i, j: (i, j)),
                   pl.BlockSpec((tm, ns), lambda i, j: (i, 0))),
        scratch_shapes=[pltpu.VMEM((tm, d), BF16)],
        compiler_params=_params("parallel", "arbitrary"),
        name="in_proj",
    )(x, g, w, w_small)


def _chunk_major_spec(tm, width, seq, **kw):
    per_seq = seq // tm
    return pl.BlockSpec((tm // S5_L, None, S5_L, width),
                        lambda i, *_: (i % per_seq, i // per_seq, 0, 0), **kw)


def _resident(shape):
    return pl.BlockSpec(shape, lambda *_: (0,) * len(shape), pipeline_mode=pl.Buffered(1))


def _mix_mlp_kernel(*refs, n_in):
    x_refs, w_refs = refs[:n_in], refs[n_in:2 * n_in]
    r_ref, g_ref, wu_ref, wd_ref, o_ref, hn_ref = refs[2 * n_in:]

    @pl.when(pl.program_id(1) == 0)
    def _():
        h = r_ref[...]
        for x_ref, w_ref in zip(x_refs, w_refs):
            h = h + _dot(x_ref[...].reshape(-1, x_ref.shape[-1]), w_ref[...])
        hn_ref[...] = _rms(h, g_ref[...]).astype(BF16)
        o_ref[...] = h

    a = jnp.maximum(_dot(hn_ref[...], wu_ref[...]), 0.0)
    o_ref[...] += _dot((a * a).astype(BF16), wd_ref[...])


def _mix_mlp(xs, ws, resid, g, w_up, w_down, *, chunk_major_seq=None, tm=512, tf=1024):
    t, d = resid.shape
    f = w_up.shape[1]
    n_in = len(xs)
    if chunk_major_seq is None:
        x_specs = [pl.BlockSpec((tm, x.shape[1]), lambda i, j: (i, 0)) for x in xs]
    else:
        x_specs = [_chunk_major_spec(tm, x.shape[-1], chunk_major_seq) for x in xs]
    return pl.pallas_call(
        functools.partial(_mix_mlp_kernel, n_in=n_in),
        out_shape=jax.ShapeDtypeStruct((t, d), F32),
        grid=(t // tm, f // tf),
        in_specs=(x_specs + [_resident(w.shape) for w in ws]
                  + [pl.BlockSpec((tm, d), lambda i, j: (i, 0)),
                     pl.BlockSpec((1, d), lambda i, j: (0, 0)),
                     pl.BlockSpec((d, tf), lambda i, j: (0, j)),
                     pl.BlockSpec((tf, d), lambda i, j: (j, 0))]),
        out_specs=pl.BlockSpec((tm, d), lambda i, j: (i, 0)),
        scratch_shapes=[pltpu.VMEM((tm, d), BF16)],
        compiler_params=_params("parallel", "arbitrary"),
        name="mix_mlp",
    )(*xs, *ws, resid, g, w_up, w_down)


def _ple_kernel(h_ref, g_ref, wg_ref, p_ref, wp_ref, g2_ref, o_ref, on_ref):
    h = h_ref[...]
    gate = _sigmoid(_dot(_rms(h, g_ref[...]).astype(BF16), wg_ref[...]))
    h_new = h + gate * _dot(p_ref[...], wp_ref[...])
    o_ref[...] = h_new
    on_ref[...] = _rms(h_new, g2_ref[...]).reshape(on_ref.shape)


def _ple(h, g, w_gate, p, w_proj, g_next, *, chunk_major=None, tm=512):
    t, d = h.shape
    pd = p.shape[1]
    if chunk_major is None:
        normed_shape, normed_spec = (t, d), pl.BlockSpec((tm, d), lambda i: (i, 0))
    else:
        bsz, seq = chunk_major
        normed_shape, normed_spec = (seq // S5_L, bsz, S5_L, d), _chunk_major_spec(tm, d, seq)
    return pl.pallas_call(
        _ple_kernel,
        out_shape=(jax.ShapeDtypeStruct((t, d), F32), jax.ShapeDtypeStruct(normed_shape, F32)),
        grid=(t // tm,),
        in_specs=[pl.BlockSpec((tm, d), lambda i: (i, 0)),
                  _resident((1, d)),
                  _resident((d, d)),
                  pl.BlockSpec((tm, pd), lambda i: (i, 0)),
                  _resident((pd, d)),
                  _resident((1, d))],
        out_specs=(pl.BlockSpec((tm, d), lambda i: (i, 0)), normed_spec),
        compiler_params=_params("parallel"),
        name="ple",
    )(h, g, w_gate, p, w_proj, g_next)


def _glu_kernel(a_ref, w_ref, b_ref, o_ref):
    a = a_ref[...]
    z = _dot(a.astype(BF16), w_ref[...]) + b_ref[...]
    o_ref[...] = (a * _sigmoid(z)).astype(BF16)


def _glu(act, w, b, *, tm=512):
    t, d = act.shape
    return pl.pallas_call(
        _glu_kernel,
        out_shape=jax.ShapeDtypeStruct((t, d), BF16),
        grid=(t // tm,),
        in_specs=[pl.BlockSpec((tm, d), lambda i: (i, 0)), _resident((d, d)), _resident((1, d))],
        out_specs=pl.BlockSpec((tm, d), lambda i: (i, 0)),
        compiler_params=_params("parallel"),
        name="glu",
    )(act, w, b)


def _row(x, t):
    return x[t:t + 1, :]


def _run_lockstep(gens):
    live = list(gens)
    while live:
        nxt = []
        for g in live:
            try:
                next(g)
                nxt.append(g)
            except StopIteration:
                pass
        live = nxt


def _hgrn_kernel(q_ref, f_ref, i_ref, g_ref, lb_ref, gn_ref, o_ref, st_ref, *, n_chunks, hb):
    c_len, d = CHUNK, HEAD_DIM

    @pl.when(pl.program_id(2) == 0)
    def _():
        st_ref[...] = jnp.zeros_like(st_ref)

    ri = lax.broadcasted_iota(jnp.int32, (c_len, d), 0)
    rt = lax.broadcasted_iota(jnp.int32, (c_len, c_len), 0)
    rs = lax.broadcasted_iota(jnp.int32, (c_len, c_len), 1)
    tril = (rs <= rt).astype(BF16)
    diag_mask = jnp.logical_and(rs <= rt, (rs // 16) == (rt // 16))
    gn = gn_ref[...]

    def one_head(hh, r0):
        rows = pl.ds(r0, c_len)
        cols = slice(hh * d, (hh + 1) * d)
        lb = lb_ref[hh]
        q = q_ref[rows, cols]
        forget = lb + (1.0 - lb) * _sigmoid(f_ref[rows, cols])
        k = 1.0 - forget
        v = i_ref[rows, cols].astype(BF16)
        cum = _dot_split(tril, jnp.log(forget))
        yield

        def side(valid, ref_row, sign, x):
            e = jnp.where(valid, sign * (cum - ref_row), 0.0)
            return jnp.where(valid, x * jnp.exp(e), 0.0)

        c31 = _row(cum, 31)
        ref_b = jnp.where(ri < 32, _row(cum, 15), _row(cum, 47))
        ref_d = jnp.where(ri < 16, _row(cum, 8),
                          jnp.where(ri < 32, _row(cum, 24),
                                    jnp.where(ri < 48, _row(cum, 40), _row(cum, 56))))
        hi16 = (ri % 32) >= 16
        q_b = side(hi16, ref_b, 1.0, q)
        k_b = side(jnp.logical_not(hi16), ref_b, -1.0, k)
        q_off = jnp.concatenate([side(ri >= 32, c31, 1.0, q),
                                 jnp.where(ri < 32, q_b, 0.0),
                                 jnp.where(ri >= 32, q_b, 0.0)], axis=1).astype(BF16)
        k_off = jnp.concatenate([side(ri < 32, c31, -1.0, k),
                                 jnp.where(ri < 32, k_b, 0.0),
                                 jnp.where(ri >= 32, k_b, 0.0)], axis=1).astype(BF16)
        q_d = (q * jnp.exp(jnp.minimum(cum - ref_d, EXP_CLAMP))).astype(BF16)
        k_d = (k * jnp.exp(jnp.minimum(ref_d - cum, EXP_CLAMP))).astype(BF16)
        s_off = _dot_nt(q_off, k_off)
        s_diag = _dot_nt(q_d, k_d)
        cum_end = _row(cum, c_len - 1)
        q_dec = (q * jnp.exp(cum)).astype(BF16)
        k_dec = (k * jnp.exp(cum_end - cum)).astype(BF16)
        st = st_ref[hh]
        out_st = _dot_nt(q_dec, st.astype(BF16))
        st_ref[hh] = st * jnp.exp(cum_end) + _dot_tn(v, k_dec)
        yield
        scores = s_off + jnp.where(diag_mask, s_diag, 0.0)
        out = _dot(scores.astype(BF16), v) + out_st
        yield

        out = _rms(out, gn) * _silu(g_ref[rows, cols])
        o_ref[rows, cols] = out.astype(o_ref.dtype)

    def chunk(c, carry):
        r0 = pl.multiple_of(c * c_len, c_len)
        _run_lockstep([one_head(hh, r0) for hh in range(hb)])
        return carry

    lax.fori_loop(0, n_chunks, chunk, 0)


def _hgrn(proj, lb, g_norm, *, bsz, seq, sb=512, hb=8):
    sb = min(sb, seq)
    h = N_HEADS
    wb = hb * HEAD_DIM

    def spec(off):
        return pl.BlockSpec((None, sb, wb), lambda b, hh, s: (b, s, hh + off // hb))

    return pl.pallas_call(
        functools.partial(_hgrn_kernel, n_chunks=sb // CHUNK, hb=hb),
        out_shape=jax.ShapeDtypeStruct((bsz, seq, h * HEAD_DIM), BF16),
        grid=(bsz, h // hb, seq // sb),
        in_specs=[spec(0), spec(h), spec(2 * h), spec(3 * h),
                  pl.BlockSpec((hb, 1, HEAD_DIM), lambda b, hh, s: (hh, 0, 0)),
                  pl.BlockSpec((1, HEAD_DIM), lambda b, hh, s: (0, 0))],
        out_specs=pl.BlockSpec((None, sb, wb), lambda b, hh, s: (b, s, hh)),
        scratch_shapes=[pltpu.VMEM((hb, HEAD_DIM, HEAD_DIM), F32)],
        compiler_params=_params("parallel", "parallel", "arbitrary"),
        name="hgrn2",
    )(proj, proj, proj, proj, lb, g_norm)


def _delta_kernel(q_ref, k_ref, v_ref, z_ref, ab_ref, cw_ref, sc_ref, gn_ref, o_ref,
                  st_ref, hist_ref, *, n_chunks, hb):
    c_len, d = CHUNK, HEAD_DIM
    head0 = pl.program_id(1) * hb

    @pl.when(pl.program_id(2) == 0)
    def _():
        st_ref[...] = jnp.zeros_like(st_ref)
        hist_ref[...] = jnp.zeros_like(hist_ref)

    rt = lax.broadcasted_iota(jnp.int32, (c_len, c_len), 0)
    rs = lax.broadcasted_iota(jnp.int32, (c_len, c_len), 1)
    causal = rs <= rt
    strict = rs < rt
    tril = causal.astype(BF16)
    ones_cc = jnp.ones((c_len, c_len), BF16)
    eye = (rs == rt).astype(F32)
    bt, bs = rt // 16, rs // 16
    m_diag = jnp.logical_and(strict, bt == bs)
    m_l1 = jnp.logical_and(bt // 2 == bs // 2, bt == bs + 1)
    m_l2 = jnp.logical_and(bt >= 2, bs < 2)
    lane = lax.broadcasted_iota(jnp.int32, (c_len, d), 1)
    gn = gn_ref[...]

    def conv(ref, which, hh, c, r0):
        cols = slice(hh * d, (hh + 1) * d)
        cw = cw_ref[hh, which]
        prev0 = pl.multiple_of(jnp.maximum(r0 - 8, 0), 8)
        prev = jnp.where(c > 0, ref[pl.ds(prev0, 8), cols], hist_ref[which, :, cols])
        win = jnp.concatenate([prev, ref[pl.ds(r0, c_len), cols]], axis=0)
        acc = win * cw[CONV_WIDTH - 1:CONV_WIDTH, :]
        for j in range(1, CONV_WIDTH):
            shifted = pltpu.roll(win, j, axis=0)
            acc = acc + shifted * cw[CONV_WIDTH - 1 - j:CONV_WIDTH - j, :]
        return _silu(acc[8:, :])

    def one_head(hh, c, r0):
        rows = pl.ds(r0, c_len)
        cols = slice(hh * d, (hh + 1) * d)
        head = head0 + hh
        neg_a = -jnp.exp(sc_ref[hh, 0:1, :])
        dt_bias = sc_ref[hh, 1:2, :]
        q = conv(q_ref, 0, hh, c, r0)
        k = conv(k_ref, 1, hh, c, r0)
        v = conv(v_ref, 2, hh, c, r0)
        q = q * lax.rsqrt(jnp.sum(q * q, axis=-1, keepdims=True) + NORM_EPS) * (d ** -0.5)
        k = k * lax.rsqrt(jnp.sum(k * k, axis=-1, keepdims=True) + NORM_EPS)
        yield

        ab = ab_ref[rows, :]
        a_col = jnp.sum(jnp.where(lane == head, ab, 0.0), axis=-1, keepdims=True)
        b_col = jnp.sum(jnp.where(lane == head + N_HEADS, ab, 0.0), axis=-1, keepdims=True)
        beta = _sigmoid(b_col)
        x = a_col + dt_bias
        log_a = neg_a * (jnp.maximum(x, 0.0) + jnp.log(1.0 + jnp.exp(-jnp.abs(x))))
        cum = _dot_split(tril, log_a)
        cum_row = _dot_split(ones_cc, jnp.where(rs >= rt, log_a[:, :c_len], 0.0))
        k_bf = k.astype(BF16)
        k_beta = k * beta
        kk = _dot_nt(k_beta.astype(BF16), k_bf)
        qk = _dot_nt(q.astype(BF16), k_bf)
        yield
        decay = jnp.exp(jnp.where(causal, cum[:, :c_len] - cum_row, 0.0))
        n_mat = jnp.where(strict, kk * decay, 0.0)

        def mm(a, b):
            return _dot(a.astype(BF16), b.astype(BF16))

        n_d = jnp.where(m_diag, n_mat, 0.0)
        p2 = mm(n_d, n_d)
        yield
        p4 = mm(p2, p2)
        a12 = mm(eye - n_d, eye + p2)
        yield
        p8 = mm(p4, p4)
        yield
        a48 = mm(eye + p4, eye + p8)
        yield
        t_d = mm(a12, a48)
        yield
        x1 = mm(t_d, jnp.where(m_l1, n_mat, 0.0))
        yield
        t_32 = t_d - mm(x1, t_d)
        yield
        x2 = mm(t_32, jnp.where(m_l2, n_mat, 0.0))
        yield
        t_inv = t_32 - mm(x2, t_32)
        yield

        e_cum = jnp.exp(cum)
        rhs = jnp.concatenate([v * beta, k_beta * e_cum], axis=1)
        sol = mm(t_inv, rhs)
        yield
        u, w = sol[:, :d], sol[:, d:]
        intra = jnp.where(causal, qk * decay, 0.0)
        cum_end = _row(cum, c_len - 1)
        q_dec = (q * e_cum).astype(BF16)
        k_dec = (k * jnp.exp(cum_end - cum)).astype(BF16)

        st = st_ref[hh]
        st_bf = st.astype(BF16)
        v_new = u - _dot(w.astype(BF16), st_bf)
        out_st = _dot(q_dec, st_bf)
        yield
        v_new_bf = v_new.astype(BF16)
        out = out_st + _dot(intra.astype(BF16), v_new_bf)
        st_ref[hh] = st * jnp.exp(cum_end) + _dot_tn(k_dec, v_new_bf)
        yield

        out = _rms(out, gn) * _silu(z_ref[rows, cols])
        o_ref[rows, cols] = out.astype(o_ref.dtype)

    def chunk(c, carry):
        r0 = pl.multiple_of(c * c_len, c_len)
        _run_lockstep([one_head(hh, c, r0) for hh in range(hb)])
        return carry

    lax.fori_loop(0, n_chunks, chunk, 0)
    last8 = pl.ds(n_chunks * c_len - 8, 8)
    hist_ref[0] = q_ref[last8, :]
    hist_ref[1] = k_ref[last8, :]
    hist_ref[2] = v_ref[last8, :]


def _delta(proj, small, conv_w, scal, g_norm, *, bsz, seq, col0, sb=512, hb=8):
    sb = min(sb, seq)
    h = N_HEADS
    wb = hb * HEAD_DIM

    def spec(off):
        return pl.BlockSpec((None, sb, wb), lambda b, hh, s: (b, s, hh + off // hb))

    return pl.pallas_call(
        functools.partial(_delta_kernel, n_chunks=sb // CHUNK, hb=hb),
        out_shape=jax.ShapeDtypeStruct((bsz, seq, h * HEAD_DIM), BF16),
        grid=(bsz, h // hb, seq // sb),
        in_specs=[spec(col0), spec(col0 + h), spec(col0 + 2 * h), spec(col0 + 3 * h),
                  pl.BlockSpec((None, sb, HEAD_DIM), lambda b, hh, s: (b, s, 0)),
                  pl.BlockSpec((hb, 3, CONV_WIDTH, HEAD_DIM), lambda b, hh, s: (hh, 0, 0, 0)),
                  pl.BlockSpec((hb, 2, HEAD_DIM), lambda b, hh, s: (hh, 0, 0)),
                  pl.BlockSpec((1, HEAD_DIM), lambda b, hh, s: (0, 0))],
        out_specs=pl.BlockSpec((None, sb, wb), lambda b, hh, s: (b, s, hh)),
        scratch_shapes=[pltpu.VMEM((hb, HEAD_DIM, HEAD_DIM), F32),
                        pltpu.VMEM((3, 8, wb), F32)],
        compiler_params=_params("parallel", "parallel", "arbitrary"),
        name="deltanet",
    )(proj, proj, proj, proj, small, conv_w, scal, g_norm)


def _cmul(ar, ai, br, bi):
    return ar * br - ai * bi, ar * bi + ai * br


def _s5_expand(a, g8):
    gs, lanes = S5_GROUP, S5_PACK * S5_GROUP
    sel = (lax.broadcasted_iota(jnp.int32, (a.shape[0], lanes), 1) // gs) == g8
    pieces = []
    for k in range(S5_L):
        src = a[:, (k // S5_PACK) * lanes:(k // S5_PACK + 1) * lanes]
        shift = ((g8 - k % S5_PACK + S5_PACK) * gs) % lanes
        pieces.append(jnp.where(sel, pltpu.roll(src, shift, axis=1), 0.0))
    return jnp.concatenate(pieces, axis=1)


def _s5_prep_kernel(are_ref, aim_ref, ldt_ref, bre_ref, bim_ref, cre_ref, cim_ref,
                    are2_ref, aim2_ref, ldt2_ref, w8_ref, m1_ref, m2_ref, lam_ref):
    def one_group(g8, carry):
        _s5_prep_group(g8, are_ref[g8], aim_ref[g8], ldt_ref[g8], bre_ref[g8], bim_ref[g8],
                       cre_ref[g8], cim_ref[g8], are2_ref[g8], aim2_ref[g8], ldt2_ref[g8],
                       w8_ref, m1_ref, m2_ref, lam_ref)
        return carry

    lax.fori_loop(0, S5_PACK, one_group, 0)


def _s5_prep_group(g8, are, aim, ldt, bre, bim, cre, cim, are2, aim2, ldt2,
                   w8_ref, m1_ref, m2_ref, lam_ref):
    p, w, l, gs = S5_STATE, S5_W, S5_L, S5_GROUP
    lanes = S5_PACK * gs
    hp = lax.Precision.HIGHEST
    dt = jnp.exp(ldt)
    a_re = jnp.broadcast_to(are, (p, w))
    a_im = jnp.broadcast_to(aim, (p, w))
    lr, li = a_re * dt, a_im * dt
    mag = jnp.exp(lr)
    lb_re, lb_im = mag * jnp.cos(li), mag * jnp.sin(li)
    den = a_re * a_re + a_im * a_im
    xr, xi = lb_re - 1.0, lb_im
    coef_re, coef_im = (xr * a_re + xi * a_im) / den, (xi * a_re - xr * a_im) / den
    bb_re, bb_im = _cmul(coef_re, coef_im, bre, bim)

    kf = (lax.broadcasted_iota(jnp.int32, (p, w), 1) // gs).astype(F32)

    def lam_pow(e):
        m = jnp.exp(e * lr)
        return m * jnp.cos(e * li), m * jnp.sin(e * li)

    e_re, e_im = _cmul(*lam_pow(kf), cre, cim)
    lhs = jnp.concatenate([bb_re[:, :gs], -bb_im[:, :gs]], axis=0)
    rhs = jnp.concatenate([e_re, e_im], axis=0)
    r0 = lax.dot_general(lhs, rhs, (((0,), (0,)), ((), ())), precision=hp,
                         preferred_element_type=F32)
    r0x = _s5_expand(r0, g8).astype(w8_ref.dtype)
    for s in range(l):
        rows = pl.ds(pl.multiple_of(s * lanes + g8 * gs, gs), gs)
        if s:
            w8_ref[rows, :s * lanes] = jnp.zeros((gs, s * lanes), w8_ref.dtype)
        w8_ref[rows, s * lanes:] = r0x[:, :(l - s) * lanes]

    st_rows = pl.ds(pl.multiple_of(g8 * 2 * p, 2 * p), 2 * p)
    d_re, d_im = _cmul(*lam_pow(float(l - 1) - kf), bb_re, bb_im)
    m1_ref[st_rows, :] = _s5_expand(jnp.concatenate([d_re, d_im], axis=0), g8).astype(m1_ref.dtype)
    f_re, f_im = _cmul(*lam_pow(kf + 1.0), cre, cim)
    m2_ref[st_rows, :] = _s5_expand(jnp.concatenate([f_re, -f_im], axis=0), g8).astype(m2_ref.dtype)

    dt2 = jnp.exp(ldt2)
    mag_l = jnp.exp(float(l) * are2 * dt2)
    ang = float(l) * aim2 * dt2
    ll_re, ll_im = mag_l * jnp.cos(ang), mag_l * jnp.sin(ang)
    first = lax.broadcasted_iota(jnp.int32, ll_im.shape, 1) < p
    lam_ref[g8] = jnp.concatenate([ll_re, jnp.where(first, -ll_im, ll_im)]
                                  + [jnp.zeros_like(ll_re)] * 6, axis=0)


def _s5_prep(a_re, a_im, log_dt, b_re, b_im, c_re, c_im):
    g, p = a_re.shape
    w, l, pk = S5_W, S5_L, S5_PACK
    wx = l * pk * S5_GROUP
    col = lambda x: x.reshape(g, p, 1)
    tile = lambda x: jnp.tile(x, (1, 1, l))
    dup = lambda x: jnp.concatenate([x, x], axis=-1).reshape(g, 1, 2 * p)
    ldt2 = jnp.broadcast_to(log_dt.reshape(g, 1, 1), (g, 1, 2 * p))
    args = (col(a_re), col(a_im), log_dt.reshape(g, 1, 1), tile(b_re), tile(b_im),
            tile(jnp.swapaxes(c_re, 1, 2)), tile(jnp.swapaxes(c_im, 1, 2)),
            dup(a_re), dup(a_im), ldt2)

    def gspec(shape):
        return pl.BlockSpec((pk,) + shape, lambda i: (i,) + (0,) * len(shape))

    def ospec(shape):
        return pl.BlockSpec((None,) + shape, lambda i: (i,) + (0,) * len(shape))

    return pl.pallas_call(
        _s5_prep_kernel,
        out_shape=(jax.ShapeDtypeStruct((g // pk, wx, wx), BF16),
                   jax.ShapeDtypeStruct((g // pk, pk * 2 * p, wx), BF16),
                   jax.ShapeDtypeStruct((g // pk, pk * 2 * p, wx), BF16),
                   jax.ShapeDtypeStruct((g, 8, 2 * p), F32)),
        grid=(g // pk,),
        in_specs=[gspec((p, 1)), gspec((p, 1)), gspec((1, 1)), gspec((p, w)), gspec((p, w)),
                  gspec((p, w)), gspec((p, w)), gspec((1, 2 * p)), gspec((1, 2 * p)),
                  gspec((1, 2 * p))],
        out_specs=(ospec((wx, wx)), ospec((pk * 2 * p, wx)), ospec((pk * 2 * p, wx)),
                   gspec((8, 2 * p))),
        compiler_params=_params("parallel"),
        name="s5_prep",
    )(*args)


def _s5_gather(u_ref, x8_ref, nbk):
    lanes = u_ref.shape[1]
    for t in range(S5_L):
        x8_ref[:, t * lanes:(t + 1) * lanes] = u_ref[pl.ds(t, nbk, stride=S5_L), :].astype(BF16)


def _s5_inc_kernel(u_ref, m1_ref, inc_ref, x8_ref):
    _s5_gather(u_ref, x8_ref, inc_ref.shape[0])
    inc_ref[...] = _dot_nt(x8_ref[...], m1_ref[...])


def _s5_inc(u, m1, *, rb=4):
    t, d = u.shape
    ngb, sp, wx = m1.shape
    lanes = d // ngb
    rows = t // rb
    nbk = rows // S5_L
    return pl.pallas_call(
        _s5_inc_kernel,
        out_shape=jax.ShapeDtypeStruct((t // S5_L, ngb * sp), F32),
        grid=(ngb, rb),
        in_specs=[pl.BlockSpec((rows, lanes), lambda i, r: (r, i)),
                  pl.BlockSpec((None, sp, wx), lambda i, r: (i, 0, 0))],
        out_specs=pl.BlockSpec((nbk, sp), lambda i, r: (r, i)),
        scratch_shapes=[pltpu.VMEM((nbk, wx), BF16)],
        compiler_params=_params("parallel", "arbitrary"),
        name="s5_inc",
    )(u, m1)


def _s5_scan_kernel(inc_ref, lam_ref, x_ref, *, bsz, n_steps):
    lam = lam_ref[...]
    a, bc = lam[0:1, :], lam[1:2, :]
    width = inc_ref.shape[2]
    sp = 2 * S5_STATE
    first = (lax.broadcasted_iota(jnp.int32, (bsz, width), 1) % sp) < S5_STATE

    def swap(v):
        return jnp.where(first, pltpu.roll(v, width - S5_STATE, axis=1),
                         pltpu.roll(v, S5_STATE, axis=1))

    def step(n, carry):
        x, xs = carry
        x_ref[n] = x
        inc = inc_ref[n]
        return a * x + bc * xs + inc, a * xs - bc * x + swap(inc)

    zero = jnp.zeros((bsz, width), F32)
    lax.fori_loop(0, n_steps, step, (zero, zero), unroll=8)


def _s5_scan(inc, lam_rows, *, bsz, wb=2048):
    n, width = inc.shape
    n_steps = n // bsz
    blk = pl.BlockSpec((n_steps, bsz, wb), lambda i: (0, 0, i))
    x = pl.pallas_call(
        functools.partial(_s5_scan_kernel, bsz=bsz, n_steps=n_steps),
        out_shape=jax.ShapeDtypeStruct((n_steps, bsz, width), F32),
        grid=(width // wb,),
        in_specs=[blk, pl.BlockSpec((8, wb), lambda i: (0, i))],
        out_specs=blk,
        compiler_params=_params("parallel"),
        name="s5_scan",
    )(inc.reshape(n_steps, bsz, width), lam_rows)
    return x.reshape(n, width)


def _s5_out_kernel(u_ref, x_ref, w8_ref, m2_ref, d_ref, o_ref, x8_ref):
    nbk = x_ref.shape[0]
    lanes = u_ref.shape[1]
    _s5_gather(u_ref, x8_ref, nbk)
    y8 = _dot(x8_ref[...], w8_ref[...]) + _dot(x_ref[...].astype(BF16), m2_ref[...])
    c0 = math.sqrt(2.0 / math.pi)
    d_skip = d_ref[...]
    for t in range(S5_L):
        rows = pl.ds(t, nbk, stride=S5_L)
        y = y8[:, t * lanes:(t + 1) * lanes] + d_skip * u_ref[rows, :]
        o_ref[rows, :] = 0.5 * y * (1.0 + jnp.tanh(c0 * (y + 0.044715 * (y * y * y))))


def _s5_out(u, x_all, w8, m2, d_skip, *, rb=4):
    t, d = u.shape
    ngb, sp, wx = m2.shape
    lanes = d // ngb
    rows = t // rb
    nbk = rows // S5_L
    return pl.pallas_call(
        _s5_out_kernel,
        out_shape=jax.ShapeDtypeStruct((t, d), F32),
        grid=(ngb, rb),
        in_specs=[pl.BlockSpec((rows, lanes), lambda i, r: (r, i)),
                  pl.BlockSpec((nbk, sp), lambda i, r: (r, i)),
                  pl.BlockSpec((None, wx, wx), lambda i, r: (i, 0, 0)),
                  pl.BlockSpec((None, sp, wx), lambda i, r: (i, 0, 0)),
                  pl.BlockSpec((1, lanes), lambda i, r: (0, i))],
        out_specs=pl.BlockSpec((rows, lanes), lambda i, r: (r, i)),
        scratch_shapes=[pltpu.VMEM((nbk, wx), BF16)],
        compiler_params=_params("parallel", "arbitrary"),
        name="s5_out",
    )(u, x_all, w8, m2, d_skip)


def _s5_act(u, a_re, a_im, b_re, b_im, c_re, c_im, d_skip, log_dt, *, bsz):
    d = u.shape[1]
    g = d // S5_GROUP
    w8, m1, m2, lam = _s5_prep(a_re, a_im, log_dt, b_re, b_im, c_re, c_im)
    lam_rows = jnp.swapaxes(lam, 0, 1).reshape(8, g * 2 * S5_STATE)
    inc = _s5_inc(u, m1)
    x_all = _s5_scan(inc, lam_rows, bsz=bsz)
    return _s5_out(u, x_all, w8, m2, d_skip.reshape(1, d).astype(F32))


def kernel(x, p, norm_mix, norm_mlp, norm_ple, w_in_e, w_out_e, hgrn_lb, g_norm_a, conv_w, a_log, dt_bias, g_norm_b, s5_a_re, s5_a_im, s5_b_re, s5_b_im, s5_c_re, s5_c_im, s5_d, s5_log_dt, w_glu, b_glu, w_out_o, w_up, w_down, w_ple_gate, w_ple_proj, final_norm):
    bsz, seq, d = x.shape
    t = bsz * seq
    depth = p.shape[0]
    heads, hd = N_HEADS, HEAD_DIM
    width = heads * hd
    main_cols = 8 * width
    lower_bounds = jnp.cumsum(jax.nn.softmax(hgrn_lb.astype(F32), axis=0), axis=0)
    row = lambda v: v.reshape(1, -1).astype(F32)

    h = x.reshape(t, d)
    p2 = p.reshape(depth, t, -1).astype(BF16)
    out = None
    for i in range(depth):
        j = i // 2
        if i % 2 == 0:
            w_in = w_in_e[j]
            w_small = jnp.pad(w_in[:, main_cols:], ((0, 0), (0, hd - 2 * heads))).astype(BF16)
            proj, small = _in_proj(h, row(norm_mix[i]), w_in[:, :main_cols].astype(BF16), w_small)
            proj = proj.reshape(bsz, seq, main_cols)
            small = small.reshape(bsz, seq, hd)
            o_a = _hgrn(proj, lower_bounds[i].reshape(heads, 1, hd), row(g_norm_a[j]),
                        bsz=bsz, seq=seq)
            cw = conv_w[j].reshape(CONV_WIDTH, 3, heads, hd).transpose(2, 1, 0, 3)
            scal = jnp.broadcast_to(jnp.stack([a_log[j], dt_bias[j]], axis=1)[:, :, None],
                                    (heads, 2, hd)).astype(F32)
            o_b = _delta(proj, small, cw, scal, row(g_norm_b[j]), bsz=bsz, seq=seq, col0=4 * heads)
            w_out = w_out_e[j].astype(BF16)
            mix_in = ([o_a.reshape(t, width), o_b.reshape(t, width)], [w_out[:width], w_out[width:]])
            mix_seq = None
        else:
            act = _s5_act(normed.reshape(t, d), s5_a_re[j], s5_a_im[j], s5_b_re[j], s5_b_im[j],
                          s5_c_re[j], s5_c_im[j], s5_d[j], s5_log_dt[j], bsz=bsz)
            glu = _glu(act, w_glu[j].astype(BF16), row(b_glu[j]))
            mix_in = ([glu.reshape(seq // S5_L, bsz, S5_L, d)], [w_out_o[j].astype(BF16)])
            mix_seq = seq
        h = _mix_mlp(*mix_in, h, row(norm_mlp[i]), w_up[i].astype(BF16), w_down[i].astype(BF16),
                     chunk_major_seq=mix_seq)
        last = i == depth - 1
        feeds_s5 = not last and (i + 1) % 2 == 1
        h, normed = _ple(h, row(norm_ple[i]), w_ple_gate[i].astype(BF16), p2[i],
                         w_ple_proj[i].astype(BF16), row(final_norm if last else norm_mix[i + 1]),
                         chunk_major=(bsz, seq) if feeds_s5 else None)
        out = normed
    return out.reshape(bsz, seq, d)
```

```python
import functools
import math

import jax
import jax.numpy as jnp
from jax import lax
from jax.experimental import pallas as pl
from jax.experimental.pallas import tpu as pltpu

F32 = jnp.float32
BF16 = jnp.bfloat16

SUBLANES = 8
NORM_EPS = 1e-6
CHUNK = 64
HEAD_DIM = 128
N_HEADS = 8
CONV_WIDTH = 4
S5_GROUP = 16
S5_STATE = 64
S5_L = 16
S5_W = S5_L * S5_GROUP
S5_PACK = 128 // S5_GROUP
EXP_CLAMP = 80.0

VMEM_LIMIT = 56 * 1024 * 1024


def _sigmoid(x):
    return 1.0 / (1.0 + jnp.exp(-x))


def _silu(x):
    return x * _sigmoid(x)


def _rms(x, g):
    return x * lax.rsqrt(jnp.mean(x * x, axis=-1, keepdims=True) + NORM_EPS) * g


def _dot(a, b):
    return jnp.dot(a, b, preferred_element_type=F32)


def _dot_nt(a, b):
    return lax.dot_general(a, b, (((1,), (1,)), ((), ())), preferred_element_type=F32)


def _dot_tn(a, b):
    return lax.dot_general(a, b, (((0,), (0,)), ((), ())), preferred_element_type=F32)


def _dot_split(a_bf, x):
    hi = x.astype(BF16)
    lo = (x - hi.astype(F32)).astype(BF16)
    return _dot(a_bf, hi) + _dot(a_bf, lo)


def _params(*sem):
    return pltpu.CompilerParams(dimension_semantics=sem, vmem_limit_bytes=VMEM_LIMIT)


def _in_proj_kernel(x_ref, g_ref, w_ref, ws_ref, o_ref, os_ref, hn_ref):
    @pl.when(pl.program_id(1) == 0)
    def _():
        hn = _rms(x_ref[...], g_ref[...]).astype(BF16)
        hn_ref[...] = hn
        os_ref[...] = _dot(hn, ws_ref[...])

    o_ref[...] = _dot(hn_ref[...], w_ref[...])


def _in_proj(x, g, w, layer, n, w_small, *, tm=1024, tn=512):
    t, d = x.shape
    tm = min(tm, t)
    ns = w_small.shape[1]
    return pl.pallas_call(
        _in_proj_kernel,
        out_shape=(jax.ShapeDtypeStruct((t, n), F32), jax.ShapeDtypeStruct((t, ns), F32)),
        grid=(t // tm, n // tn),
        in_specs=[pl.BlockSpec((tm, d), lambda i, j: (i, 0)),
                  pl.BlockSpec((1, d), lambda i, j: (0, 0)),
                  pl.BlockSpec((None, d, tn), lambda i, j: (layer, 0, j)),
                  pl.BlockSpec((d, ns), lambda i, j: (0, 0))],
        out_specs=(pl.BlockSpec((tm, tn), lambda i, j: (i, j)),
                   pl.BlockSpec((tm, ns), lambda i, j: (i, 0))),
        scratch_shapes=[pltpu.VMEM((tm, d), BF16)],
        compiler_params=_params("parallel", "arbitrary"),
        name="in_proj",
    )(x, g, w, w_small)


def _chunk_major_spec(tm, width, seq, **kw):
    per_seq = seq // tm
    return pl.BlockSpec((tm // S5_L, None, S5_L, width),
                        lambda i, *_: (i % per_seq, i // per_seq, 0, 0), **kw)


def _resident(shape, layer=None, block=0):
    if layer is None:
        return pl.BlockSpec(shape, lambda *_: (0,) * len(shape), pipeline_mode=pl.Buffered(1))
    return pl.BlockSpec((None,) + tuple(shape), lambda *_: (layer, block, 0),
                        pipeline_mode=pl.Buffered(1))


def _mix_mlp_kernel(*refs, n_in):
    x_refs, w_refs = refs[:n_in], refs[n_in:2 * n_in]
    r_ref, g_ref, wu_ref, wd_ref, o_ref, hn_ref = refs[2 * n_in:]

    @pl.when(pl.program_id(1) == 0)
    def _():
        h = r_ref[...]
        for x_ref, w_ref in zip(x_refs, w_refs):
            h = h + _dot(x_ref[...].reshape(-1, x_ref.shape[-1]), w_ref[...])
        hn_ref[...] = _rms(h, g_ref[...]).astype(BF16)
        o_ref[...] = h

    a = jnp.maximum(_dot(hn_ref[...], wu_ref[...]), 0.0)
    o_ref[...] += _dot((a * a).astype(BF16), wd_ref[...])


def _mix_mlp(xs, w_mix, mix_layer, resid, g, w_up, w_down, layer, *, chunk_major_seq=None,
             tm=512, tf=1024):
    t, d = resid.shape
    f = w_up.shape[2]
    n_in = len(xs)
    kx = w_mix.shape[1] // n_in
    if chunk_major_seq is None:
        x_specs = [pl.BlockSpec((tm, x.shape[1]), lambda i, j: (i, 0)) for x in xs]
    else:
        x_specs = [_chunk_major_spec(tm, x.shape[-1], chunk_major_seq) for x in xs]
    return pl.pallas_call(
        functools.partial(_mix_mlp_kernel, n_in=n_in),
        out_shape=jax.ShapeDtypeStruct((t, d), F32),
        grid=(t // tm, f // tf),
        in_specs=(x_specs + [_resident((kx, d), mix_layer, k) for k in range(n_in)]
                  + [pl.BlockSpec((tm, d), lambda i, j: (i, 0)),
                     pl.BlockSpec((1, d), lambda i, j: (0, 0)),
                     pl.BlockSpec((None, d, tf), lambda i, j: (layer, 0, j)),
                     pl.BlockSpec((None, tf, d), lambda i, j: (layer, j, 0))]),
        out_specs=pl.BlockSpec((tm, d), lambda i, j: (i, 0)),
        scratch_shapes=[pltpu.VMEM((tm, d), BF16)],
        compiler_params=_params("parallel", "arbitrary"),
        name="mix_mlp",
    )(*xs, *([w_mix] * n_in), resid, g, w_up, w_down)


def _ple_kernel(h_ref, g_ref, wg_ref, p_ref, wp_ref, g2_ref, o_ref, on_ref):
    h = h_ref[...]
    gate = _sigmoid(_dot(_rms(h, g_ref[...]).astype(BF16), wg_ref[...]))
    h_new = h + gate * _dot(p_ref[...], wp_ref[...])
    o_ref[...] = h_new
    on_ref[...] = _rms(h_new, g2_ref[...]).reshape(on_ref.shape)


def _ple(h, g, w_gate, p, w_proj, g_next, layer, *, chunk_major=None, tm=512):
    t, d = h.shape
    pd = p.shape[2]
    if chunk_major is None:
        normed_shape, normed_spec = (t, d), pl.BlockSpec((tm, d), lambda i: (i, 0))
    else:
        bsz, seq = chunk_major
        normed_shape, normed_spec = (seq // S5_L, bsz, S5_L, d), _chunk_major_spec(tm, d, seq)
    return pl.pallas_call(
        _ple_kernel,
        out_shape=(jax.ShapeDtypeStruct((t, d), F32), jax.ShapeDtypeStruct(normed_shape, F32)),
        grid=(t // tm,),
        in_specs=[pl.BlockSpec((tm, d), lambda i: (i, 0)),
                  _resident((1, d)),
                  _resident((d, d), layer),
                  pl.BlockSpec((None, tm, pd), lambda i: (layer, i, 0)),
                  _resident((pd, d), layer),
                  _resident((1, d))],
        out_specs=(pl.BlockSpec((tm, d), lambda i: (i, 0)), normed_spec),
        compiler_params=_params("parallel"),
        name="ple",
    )(h, g, w_gate, p, w_proj, g_next)


def _glu_kernel(a_ref, w_ref, b_ref, o_ref):
    a = a_ref[...]
    z = _dot(a.astype(BF16), w_ref[...]) + b_ref[...]
    o_ref[...] = (a * _sigmoid(z)).astype(BF16)


def _glu(act, w, layer, b, *, tm=512):
    t, d = act.shape
    return pl.pallas_call(
        _glu_kernel,
        out_shape=jax.ShapeDtypeStruct((t, d), BF16),
        grid=(t // tm,),
        in_specs=[pl.BlockSpec((tm, d), lambda i: (i, 0)), _resident((d, d), layer),
                  _resident((1, d))],
        out_specs=pl.BlockSpec((tm, d), lambda i: (i, 0)),
        compiler_params=_params("parallel"),
        name="glu",
    )(act, w, b)


def _row(x, t):
    return x[t:t + 1, :]


def _run_lockstep(gens):
    live = list(gens)
    while live:
        nxt = []
        for g in live:
            try:
                next(g)
                nxt.append(g)
            except StopIteration:
                pass
        live = nxt


def _hgrn_kernel(q_ref, f_ref, i_ref, g_ref, lb_ref, gn_ref, o_ref, st_ref, sc_ref, ost_ref, vb_ref,
                 *, n_chunks, hb):
    c_len, d = CHUNK, HEAD_DIM

    @pl.when(pl.program_id(2) == 0)
    def _():
        st_ref[...] = jnp.zeros_like(st_ref)

    ri = lax.broadcasted_iota(jnp.int32, (c_len, d), 0)
    rt = lax.broadcasted_iota(jnp.int32, (c_len, c_len), 0)
    rs = lax.broadcasted_iota(jnp.int32, (c_len, c_len), 1)
    tril = (rs <= rt).astype(BF16)
    diag_mask = jnp.logical_and(rs <= rt, (rs // 16) == (rt // 16))
    gn = gn_ref[...]

    def phase_a(hh, r0, slot):
        rows = pl.ds(r0, c_len)
        cols = slice(hh * d, (hh + 1) * d)
        lb = lb_ref[hh]
        q = q_ref[rows, cols]
        forget = lb + (1.0 - lb) * _sigmoid(f_ref[rows, cols])
        k = 1.0 - forget
        v = i_ref[rows, cols].astype(BF16)
        cum = _dot_split(tril, jnp.log(forget))
        yield

        def side(valid, ref_row, sign, x):
            e = jnp.where(valid, sign * (cum - ref_row), 0.0)
            return jnp.where(valid, x * jnp.exp(e), 0.0)

        c31 = _row(cum, 31)
        ref_b = jnp.where(ri < 32, _row(cum, 15), _row(cum, 47))
        ref_d = jnp.where(ri < 16, _row(cum, 8),
                          jnp.where(ri < 32, _row(cum, 24),
                                    jnp.where(ri < 48, _row(cum, 40), _row(cum, 56))))
        hi16 = (ri % 32) >= 16
        q_b = side(hi16, ref_b, 1.0, q)
        k_b = side(jnp.logical_not(hi16), ref_b, -1.0, k)
        q_off = jnp.concatenate([side(ri >= 32, c31, 1.0, q),
                                 jnp.where(ri < 32, q_b, 0.0),
                                 jnp.where(ri >= 32, q_b, 0.0)], axis=1).astype(BF16)
        k_off = jnp.concatenate([side(ri < 32, c31, -1.0, k),
                                 jnp.where(ri < 32, k_b, 0.0),
                                 jnp.where(ri >= 32, k_b, 0.0)], axis=1).astype(BF16)
        q_d = (q * jnp.exp(jnp.minimum(cum - ref_d, EXP_CLAMP))).astype(BF16)
        k_d = (k * jnp.exp(jnp.minimum(ref_d - cum, EXP_CLAMP))).astype(BF16)
        s_off = _dot_nt(q_off, k_off)
        s_diag = _dot_nt(q_d, k_d)
        cum_end = _row(cum, c_len - 1)
        q_dec = (q * jnp.exp(cum)).astype(BF16)
        k_dec = (k * jnp.exp(cum_end - cum)).astype(BF16)
        st = st_ref[hh]
        out_st = _dot_nt(q_dec, st.astype(BF16))
        st_new = st * jnp.exp(cum_end) + _dot_tn(v, k_dec)
        yield
        st_ref[hh] = st_new
        sc_ref[slot, hh] = (s_off + jnp.where(diag_mask, s_diag, 0.0)).astype(BF16)
        ost_ref[slot, hh] = out_st
        vb_ref[slot, hh] = v

    def phase_b(hh, r0, slot):
        rows = pl.ds(r0, c_len)
        cols = slice(hh * d, (hh + 1) * d)
        out = _dot(sc_ref[slot, hh], vb_ref[slot, hh]) + ost_ref[slot, hh]
        yield
        out = _rms(out, gn) * _silu(g_ref[rows, cols])
        o_ref[rows, cols] = out.astype(o_ref.dtype)

    def first_chunk(c, carry):
        _run_lockstep([phase_a(hh, pl.multiple_of(c * c_len, c_len), c % 2) for hh in range(hb)])
        return carry

    def chunk(c, carry):
        r0 = pl.multiple_of(c * c_len, c_len)
        slot = c % 2
        _run_lockstep([phase_b(hh, pl.multiple_of(r0 - c_len, c_len), 1 - slot) for hh in range(hb)]
                      + [phase_a(hh, r0, slot) for hh in range(hb)])
        return carry

    lax.fori_loop(0, 1, first_chunk, 0)
    lax.fori_loop(1, n_chunks, chunk, 0)
    _run_lockstep([phase_b(hh, (n_chunks - 1) * c_len, (n_chunks - 1) % 2) for hh in range(hb)])


def _hgrn(proj, lb, g_norm, *, bsz, seq, sb=512, hb=8):
    sb = min(sb, seq)
    h = N_HEADS
    wb = hb * HEAD_DIM

    def spec(off):
        return pl.BlockSpec((None, sb, wb), lambda b, hh, s: (b, s, hh + off // hb))

    return pl.pallas_call(
        functools.partial(_hgrn_kernel, n_chunks=sb // CHUNK, hb=hb),
        out_shape=jax.ShapeDtypeStruct((bsz, seq, h * HEAD_DIM), BF16),
        grid=(bsz, h // hb, seq // sb),
        in_specs=[spec(0), spec(h), spec(2 * h), spec(3 * h),
                  pl.BlockSpec((hb, 1, HEAD_DIM), lambda b, hh, s: (hh, 0, 0)),
                  pl.BlockSpec((1, HEAD_DIM), lambda b, hh, s: (0, 0))],
        out_specs=pl.BlockSpec((None, sb, wb), lambda b, hh, s: (b, s, hh)),
        scratch_shapes=[pltpu.VMEM((hb, HEAD_DIM, HEAD_DIM), F32),
                        pltpu.VMEM((2, hb, CHUNK, CHUNK), BF16),
                        pltpu.VMEM((2, hb, CHUNK, HEAD_DIM), F32),
                        pltpu.VMEM((2, hb, CHUNK, HEAD_DIM), BF16)],
        compiler_params=_params("parallel", "parallel", "arbitrary"),
        name="hgrn2",
    )(proj, proj, proj, proj, lb, g_norm)


def _delta_kernel(q_ref, k_ref, v_ref, z_ref, ab_ref, cw_ref, sc_ref, gn_ref, o_ref,
                  st_ref, hist_ref, td_ref, nm_ref, rhs_ref, in_ref, qd_ref, kd_ref, dec_ref,
                  *, n_chunks, hb):
    c_len, d = CHUNK, HEAD_DIM
    head0 = pl.program_id(1) * hb

    @pl.when(pl.program_id(2) == 0)
    def _():
        st_ref[...] = jnp.zeros_like(st_ref)
        hist_ref[...] = jnp.zeros_like(hist_ref)

    rt = lax.broadcasted_iota(jnp.int32, (c_len, c_len), 0)
    rs = lax.broadcasted_iota(jnp.int32, (c_len, c_len), 1)
    causal = rs <= rt
    strict = rs < rt
    tril = causal.astype(BF16)
    ones_cc = jnp.ones((c_len, c_len), BF16)
    eye = (rs == rt).astype(F32)
    bt, bs = rt // 16, rs // 16
    m_diag = jnp.logical_and(strict, bt == bs)
    m_l1 = jnp.logical_and(bt // 2 == bs // 2, bt == bs + 1)
    m_l2 = jnp.logical_and(bt >= 2, bs < 2)
    lane = lax.broadcasted_iota(jnp.int32, (c_len, d), 1)
    gn = gn_ref[...]

    def conv(ref, which, hh, c, r0):
        cols = slice(hh * d, (hh + 1) * d)
        cw = cw_ref[hh, which]
        prev0 = pl.multiple_of(jnp.maximum(r0 - 8, 0), 8)
        prev = jnp.where(c > 0, ref[pl.ds(prev0, 8), cols], hist_ref[which, :, cols])
        win = jnp.concatenate([prev, ref[pl.ds(r0, c_len), cols]], axis=0)
        acc = win * cw[CONV_WIDTH - 1:CONV_WIDTH, :]
        for j in range(1, CONV_WIDTH):
            shifted = pltpu.roll(win, j, axis=0)
            acc = acc + shifted * cw[CONV_WIDTH - 1 - j:CONV_WIDTH - j, :]
        return _silu(acc[8:, :])

    def mm(a, b):
        return _dot(a.astype(BF16), b.astype(BF16))

    def phase_a(hh, c, r0, slot):
        rows = pl.ds(r0, c_len)
        cols = slice(hh * d, (hh + 1) * d)
        head = head0 + hh
        neg_a = -jnp.exp(sc_ref[hh, 0:1, :])
        dt_bias = sc_ref[hh, 1:2, :]
        q = conv(q_ref, 0, hh, c, r0)
        k = conv(k_ref, 1, hh, c, r0)
        v = conv(v_ref, 2, hh, c, r0)
        q = q * lax.rsqrt(jnp.sum(q * q, axis=-1, keepdims=True) + NORM_EPS) * (d ** -0.5)
        k = k * lax.rsqrt(jnp.sum(k * k, axis=-1, keepdims=True) + NORM_EPS)
        yield

        ab = ab_ref[rows, :]
        a_col = jnp.sum(jnp.where(lane == head, ab, 0.0), axis=-1, keepdims=True)
        b_col = jnp.sum(jnp.where(lane == head + N_HEADS, ab, 0.0), axis=-1, keepdims=True)
        beta = _sigmoid(b_col)
        x = a_col + dt_bias
        log_a = neg_a * (jnp.maximum(x, 0.0) + jnp.log(1.0 + jnp.exp(-jnp.abs(x))))
        cum = _dot_split(tril, log_a)
        cum_row = _dot_split(ones_cc, jnp.where(rs >= rt, log_a[:, :c_len], 0.0))
        k_bf = k.astype(BF16)
        k_beta = k * beta
        kk = _dot_nt(k_beta.astype(BF16), k_bf)
        qk = _dot_nt(q.astype(BF16), k_bf)
        yield
        decay = jnp.exp(jnp.where(causal, cum[:, :c_len] - cum_row, 0.0))
        n_mat = jnp.where(strict, kk * decay, 0.0)

        n_d = jnp.where(m_diag, n_mat, 0.0)
        p2 = mm(n_d, n_d)
        e_cum = jnp.exp(cum)
        cum_end = _row(cum, c_len - 1)
        rhs_ref[slot, hh] = jnp.concatenate([v * beta, k_beta * e_cum], axis=1).astype(BF16)
        qd_ref[slot, hh] = (q * e_cum).astype(BF16)
        kd_ref[slot, hh] = (k * jnp.exp(cum_end - cum)).astype(BF16)
        dec_ref[slot, hh] = jnp.exp(cum_end)
        in_ref[slot, hh] = jnp.where(causal, qk * decay, 0.0).astype(BF16)
        nm_ref[slot, hh] = n_mat
        yield
        p4 = mm(p2, p2)
        a12 = mm(eye - n_d, eye + p2)
        yield
        p8 = mm(p4, p4)
        yield
        a48 = mm(eye + p4, eye + p8)
        yield
        td_ref[slot, hh] = mm(a12, a48)

    def phase_b(hh, r0, slot):
        rows = pl.ds(r0, c_len)
        cols = slice(hh * d, (hh + 1) * d)
        t_d = td_ref[slot, hh]
        n_mat = nm_ref[slot, hh]
        x1 = mm(t_d, jnp.where(m_l1, n_mat, 0.0))
        yield
        t_32 = t_d - mm(x1, t_d)
        yield
        x2 = mm(t_32, jnp.where(m_l2, n_mat, 0.0))
        yield
        t_inv = t_32 - mm(x2, t_32)
        yield
        sol = _dot(t_inv.astype(BF16), rhs_ref[slot, hh])
        yield
        u, w = sol[:, :d], sol[:, d:]
        st = st_ref[hh]
        st_bf = st.astype(BF16)
        v_new = u - _dot(w.astype(BF16), st_bf)
        out_st = _dot(qd_ref[slot, hh], st_bf)
        yield
        v_new_bf = v_new.astype(BF16)
        out = out_st + _dot(in_ref[slot, hh], v_new_bf)
        st_ref[hh] = st * dec_ref[slot, hh] + _dot_tn(kd_ref[slot, hh], v_new_bf)
        yield
        out = _rms(out, gn) * _silu(z_ref[rows, cols])
        o_ref[rows, cols] = out.astype(o_ref.dtype)

    def first_chunk(c, carry):
        _run_lockstep([phase_a(hh, c, pl.multiple_of(c * c_len, c_len), c % 2) for hh in range(hb)])
        return carry

    def chunk(c, carry):
        r0 = pl.multiple_of(c * c_len, c_len)
        slot = c % 2
        _run_lockstep([phase_b(hh, pl.multiple_of(r0 - c_len, c_len), 1 - slot) for hh in range(hb)]
                      + [phase_a(hh, c, r0, slot) for hh in range(hb)])
        return carry

    lax.fori_loop(0, 1, first_chunk, 0)
    lax.fori_loop(1, n_chunks, chunk, 0)
    _run_lockstep([phase_b(hh, (n_chunks - 1) * c_len, (n_chunks - 1) % 2) for hh in range(hb)])
    last8 = pl.ds(n_chunks * c_len - 8, 8)
    hist_ref[0] = q_ref[last8, :]
    hist_ref[1] = k_ref[last8, :]
    hist_ref[2] = v_ref[last8, :]


def _delta(proj, small, conv_w, scal, g_norm, *, bsz, seq, col0, sb=512, hb=8):
    sb = min(sb, seq)
    h = N_HEADS
    wb = hb * HEAD_DIM

    def spec(off):
        return pl.BlockSpec((None, sb, wb), lambda b, hh, s: (b, s, hh + off // hb))

    return pl.pallas_call(
        functools.partial(_delta_kernel, n_chunks=sb // CHUNK, hb=hb),
        out_shape=jax.ShapeDtypeStruct((bsz, seq, h * HEAD_DIM), BF16),
        grid=(bsz, h // hb, seq // sb),
        in_specs=[spec(col0), spec(col0 + h), spec(col0 + 2 * h), spec(col0 + 3 * h),
                  pl.BlockSpec((None, sb, HEAD_DIM), lambda b, hh, s: (b, s, 0)),
                  pl.BlockSpec((hb, 3, CONV_WIDTH, HEAD_DIM), lambda b, hh, s: (hh, 0, 0, 0)),
                  pl.BlockSpec((hb, 2, HEAD_DIM), lambda b, hh, s: (hh, 0, 0)),
                  pl.BlockSpec((1, HEAD_DIM), lambda b, hh, s: (0, 0))],
        out_specs=pl.BlockSpec((None, sb, wb), lambda b, hh, s: (b, s, hh)),
        scratch_shapes=[pltpu.VMEM((hb, HEAD_DIM, HEAD_DIM), F32),
                        pltpu.VMEM((3, 8, wb), F32),
                        pltpu.VMEM((2, hb, CHUNK, CHUNK), F32),
                        pltpu.VMEM((2, hb, CHUNK, CHUNK), F32),
                        pltpu.VMEM((2, hb, CHUNK, 2 * HEAD_DIM), BF16),
                        pltpu.VMEM((2, hb, CHUNK, CHUNK), BF16),
                        pltpu.VMEM((2, hb, CHUNK, HEAD_DIM), BF16),
                        pltpu.VMEM((2, hb, CHUNK, HEAD_DIM), BF16),
                        pltpu.VMEM((2, hb, 1, HEAD_DIM), F32)],
        compiler_params=_params("parallel", "parallel", "arbitrary"),
        name="deltanet",
    )(proj, proj, proj, proj, small, conv_w, scal, g_norm)


def _cmul(ar, ai, br, bi):
    return ar * br - ai * bi, ar * bi + ai * br


def _s5_expand(a, g8):
    gs, lanes = S5_GROUP, S5_PACK * S5_GROUP
    sel = (lax.broadcasted_iota(jnp.int32, (a.shape[0], lanes), 1) // gs) == g8
    pieces = []
    for k in range(S5_L):
        src = a[:, (k // S5_PACK) * lanes:(k // S5_PACK + 1) * lanes]
        shift = ((g8 - k % S5_PACK + S5_PACK) * gs) % lanes
        pieces.append(jnp.where(sel, pltpu.roll(src, shift, axis=1), 0.0))
    return jnp.concatenate(pieces, axis=1)


def _s5_prep_kernel(are_ref, aim_ref, ldt_ref, bre_ref, bim_ref, cre_ref, cim_ref,
                    are2_ref, aim2_ref, ldt2_ref, w8_ref, m1_ref, m2_ref, lam_ref):
    def one_group(g8, carry):
        _s5_prep_group(g8, are_ref[g8], aim_ref[g8], ldt_ref[g8], bre_ref[g8], bim_ref[g8],
                       cre_ref[g8], cim_ref[g8], are2_ref[g8], aim2_ref[g8], ldt2_ref[g8],
                       w8_ref, m1_ref, m2_ref, lam_ref)
        return carry

    lax.fori_loop(0, S5_PACK, one_group, 0)


def _s5_prep_group(g8, are, aim, ldt, bre, bim, cre, cim, are2, aim2, ldt2,
                   w8_ref, m1_ref, m2_ref, lam_ref):
    p, w, l, gs = S5_STATE, S5_W, S5_L, S5_GROUP
    lanes = S5_PACK * gs
    hp = lax.Precision.HIGHEST
    dt = jnp.exp(ldt)
    a_re = jnp.broadcast_to(are, (p, w))
    a_im = jnp.broadcast_to(aim, (p, w))
    lr, li = a_re * dt, a_im * dt
    mag = jnp.exp(lr)
    lb_re, lb_im = mag * jnp.cos(li), mag * jnp.sin(li)
    den = a_re * a_re + a_im * a_im
    xr, xi = lb_re - 1.0, lb_im
    coef_re, coef_im = (xr * a_re + xi * a_im) / den, (xi * a_re - xr * a_im) / den
    bb_re, bb_im = _cmul(coef_re, coef_im, bre, bim)

    kf = (lax.broadcasted_iota(jnp.int32, (p, w), 1) // gs).astype(F32)

    def lam_pow(e):
        m = jnp.exp(e * lr)
        return m * jnp.cos(e * li), m * jnp.sin(e * li)

    e_re, e_im = _cmul(*lam_pow(kf), cre, cim)
    lhs = jnp.concatenate([bb_re[:, :gs], -bb_im[:, :gs]], axis=0)
    rhs = jnp.concatenate([e_re, e_im], axis=0)
    r0 = lax.dot_general(lhs, rhs, (((0,), (0,)), ((), ())), precision=hp,
                         preferred_element_type=F32)
    r0x = _s5_expand(r0, g8).astype(w8_ref.dtype)
    for s in range(l):
        rows = pl.ds(pl.multiple_of(s * lanes + g8 * gs, gs), gs)
        if s:
            w8_ref[rows, :s * lanes] = jnp.zeros((gs, s * lanes), w8_ref.dtype)
        w8_ref[rows, s * lanes:] = r0x[:, :(l - s) * lanes]

    st_rows = pl.ds(pl.multiple_of(g8 * 2 * p, 2 * p), 2 * p)
    d_re, d_im = _cmul(*lam_pow(float(l - 1) - kf), bb_re, bb_im)
    m1_ref[st_rows, :] = _s5_expand(jnp.concatenate([d_re, d_im], axis=0), g8).astype(m1_ref.dtype)
    f_re, f_im = _cmul(*lam_pow(kf + 1.0), cre, cim)
    m2_ref[st_rows, :] = _s5_expand(jnp.concatenate([f_re, -f_im], axis=0), g8).astype(m2_ref.dtype)

    dt2 = jnp.exp(ldt2)
    mag_l = jnp.exp(float(l) * are2 * dt2)
    ang = float(l) * aim2 * dt2
    ll_re, ll_im = mag_l * jnp.cos(ang), mag_l * jnp.sin(ang)
    first = lax.broadcasted_iota(jnp.int32, ll_im.shape, 1) < p
    lam_ref[g8] = jnp.concatenate([ll_re, jnp.where(first, -ll_im, ll_im)]
                                  + [jnp.zeros_like(ll_re)] * 6, axis=0)


def _s5_prep(a_re, a_im, log_dt, b_re, b_im, c_re, c_im):
    g, p = a_re.shape
    w, l, pk = S5_W, S5_L, S5_PACK
    wx = l * pk * S5_GROUP
    col = lambda x: x.reshape(g, p, 1)
    tile = lambda x: jnp.tile(x, (1, 1, l))
    dup = lambda x: jnp.concatenate([x, x], axis=-1).reshape(g, 1, 2 * p)
    ldt2 = jnp.broadcast_to(log_dt.reshape(g, 1, 1), (g, 1, 2 * p))
    args = (col(a_re), col(a_im), log_dt.reshape(g, 1, 1), tile(b_re), tile(b_im),
            tile(jnp.swapaxes(c_re, 1, 2)), tile(jnp.swapaxes(c_im, 1, 2)),
            dup(a_re), dup(a_im), ldt2)

    def gspec(shape):
        return pl.BlockSpec((pk,) + shape, lambda i: (i,) + (0,) * len(shape))

    def ospec(shape):
        return pl.BlockSpec((None,) + shape, lambda i: (i,) + (0,) * len(shape))

    return pl.pallas_call(
        _s5_prep_kernel,
        out_shape=(jax.ShapeDtypeStruct((g // pk, wx, wx), BF16),
                   jax.ShapeDtypeStruct((g // pk, pk * 2 * p, wx), BF16),
                   jax.ShapeDtypeStruct((g // pk, pk * 2 * p, wx), BF16),
                   jax.ShapeDtypeStruct((g, 8, 2 * p), F32)),
        grid=(g // pk,),
        in_specs=[gspec((p, 1)), gspec((p, 1)), gspec((1, 1)), gspec((p, w)), gspec((p, w)),
                  gspec((p, w)), gspec((p, w)), gspec((1, 2 * p)), gspec((1, 2 * p)),
                  gspec((1, 2 * p))],
        out_specs=(ospec((wx, wx)), ospec((pk * 2 * p, wx)), ospec((pk * 2 * p, wx)),
                   gspec((8, 2 * p))),
        compiler_params=_params("parallel"),
        name="s5_prep",
    )(*args)


def _s5_gather(u_ref, x8_ref, nbk):
    lanes = u_ref.shape[1]
    for t in range(S5_L):
        x8_ref[:, t * lanes:(t + 1) * lanes] = u_ref[pl.ds(t, nbk, stride=S5_L), :].astype(BF16)


def _s5_inc_kernel(u_ref, m1_ref, inc_ref, x8_ref):
    _s5_gather(u_ref, x8_ref, inc_ref.shape[0])
    inc_ref[...] = _dot_nt(x8_ref[...], m1_ref[...])


def _s5_inc(u, m1, *, rb=4):
    t, d = u.shape
    ngb, sp, wx = m1.shape
    lanes = d // ngb
    rows = t // rb
    nbk = rows // S5_L
    return pl.pallas_call(
        _s5_inc_kernel,
        out_shape=jax.ShapeDtypeStruct((t // S5_L, ngb * sp), F32),
        grid=(ngb, rb),
        in_specs=[pl.BlockSpec((rows, lanes), lambda i, r: (r, i)),
                  pl.BlockSpec((None, sp, wx), lambda i, r: (i, 0, 0))],
        out_specs=pl.BlockSpec((nbk, sp), lambda i, r: (r, i)),
        scratch_shapes=[pltpu.VMEM((nbk, wx), BF16)],
        compiler_params=_params("parallel", "arbitrary"),
        name="s5_inc",
    )(u, m1)


def _s5_scan_kernel(inc_ref, lam_ref, x_ref, *, bsz, n_steps):
    lam = lam_ref[...]
    a, bc = lam[0:1, :], lam[1:2, :]
    width = inc_ref.shape[1]
    sp = 2 * S5_STATE
    per = SUBLANES // bsz
    first = (lax.broadcasted_iota(jnp.int32, (SUBLANES, width), 1) % sp) < S5_STATE
    row = lax.broadcasted_iota(jnp.int32, (SUBLANES, width), 0)

    def swap(v):
        return jnp.where(first, pltpu.roll(v, width - S5_STATE, axis=1),
                         pltpu.roll(v, S5_STATE, axis=1))

    def tile_step(m, carry):
        x, xs = carry
        rows = pl.ds(pl.multiple_of(m * SUBLANES, SUBLANES), SUBLANES)
        inc_tile = inc_ref[rows, :]
        out = x
        for j in range(per):
            inc = inc_tile if j == 0 else pltpu.roll(inc_tile, SUBLANES - j * bsz, axis=0)
            x, xs = a * x + bc * xs + inc, a * xs - bc * x + swap(inc)
            if j + 1 < per:
                out = jnp.where(row < (j + 1) * bsz, out, pltpu.roll(x, (j + 1) * bsz, axis=0))
        x_ref[rows, :] = out
        return x, xs

    zero = jnp.zeros((SUBLANES, width), F32)
    lax.fori_loop(0, n_steps // per, tile_step, (zero, zero), unroll=4)


def _s5_scan(inc, lam_rows, *, bsz, wb=2048):
    n, width = inc.shape
    blk = pl.BlockSpec((n, wb), lambda i: (0, i))
    return pl.pallas_call(
        functools.partial(_s5_scan_kernel, bsz=bsz, n_steps=n // bsz),
        out_shape=jax.ShapeDtypeStruct((n, width), F32),
        grid=(width // wb,),
        in_specs=[blk, pl.BlockSpec((8, wb), lambda i: (0, i))],
        out_specs=blk,
        compiler_params=_params("parallel"),
        name="s5_scan",
    )(inc, lam_rows)


def _s5_out_kernel(u_ref, x_ref, w8_ref, m2_ref, d_ref, o_ref, x8_ref):
    nbk = x_ref.shape[0]
    lanes = u_ref.shape[1]
    _s5_gather(u_ref, x8_ref, nbk)
    y8 = _dot(x8_ref[...], w8_ref[...]) + _dot(x_ref[...].astype(BF16), m2_ref[...])
    c0 = math.sqrt(2.0 / math.pi)
    d_skip = d_ref[...]
    for t in range(S5_L):
        rows = pl.ds(t, nbk, stride=S5_L)
        y = y8[:, t * lanes:(t + 1) * lanes] + d_skip * u_ref[rows, :]
        o_ref[rows, :] = 0.5 * y * (1.0 + jnp.tanh(c0 * (y + 0.044715 * (y * y * y))))


def _s5_out(u, x_all, w8, m2, d_skip, *, rb=4):
    t, d = u.shape
    ngb, sp, wx = m2.shape
    lanes = d // ngb
    rows = t // rb
    nbk = rows // S5_L
    return pl.pallas_call(
        _s5_out_kernel,
        out_shape=jax.ShapeDtypeStruct((t, d), F32),
        grid=(ngb, rb),
        in_specs=[pl.BlockSpec((rows, lanes), lambda i, r: (r, i)),
                  pl.BlockSpec((nbk, sp), lambda i, r: (r, i)),
                  pl.BlockSpec((None, wx, wx), lambda i, r: (i, 0, 0)),
                  pl.BlockSpec((None, sp, wx), lambda i, r: (i, 0, 0)),
                  pl.BlockSpec((1, lanes), lambda i, r: (0, i))],
        out_specs=pl.BlockSpec((rows, lanes), lambda i, r: (r, i)),
        scratch_shapes=[pltpu.VMEM((nbk, wx), BF16)],
        compiler_params=_params("parallel", "arbitrary"),
        name="s5_out",
    )(u, x_all, w8, m2, d_skip)


def _s5_act(u, a_re, a_im, b_re, b_im, c_re, c_im, d_skip, log_dt, *, bsz):
    d = u.shape[1]
    g = d // S5_GROUP
    w8, m1, m2, lam = _s5_prep(a_re, a_im, log_dt, b_re, b_im, c_re, c_im)
    lam_rows = jnp.swapaxes(lam, 0, 1).reshape(8, g * 2 * S5_STATE)
    inc = _s5_inc(u, m1)
    x_all = _s5_scan(inc, lam_rows, bsz=bsz)
    return _s5_out(u, x_all, w8, m2, d_skip.reshape(1, d).astype(F32))


def kernel(x, p, norm_mix, norm_mlp, norm_ple, w_in_e, w_out_e, hgrn_lb, g_norm_a, conv_w, a_log, dt_bias, g_norm_b, s5_a_re, s5_a_im, s5_b_re, s5_b_im, s5_c_re, s5_c_im, s5_d, s5_log_dt, w_glu, b_glu, w_out_o, w_up, w_down, w_ple_gate, w_ple_proj, final_norm):
    bsz, seq, d = x.shape
    t = bsz * seq
    depth = p.shape[0]
    heads, hd = N_HEADS, HEAD_DIM
    width = heads * hd
    main_cols = 8 * width
    lower_bounds = jnp.cumsum(jax.nn.softmax(hgrn_lb.astype(F32), axis=0), axis=0)
    row = lambda v: v.reshape(1, -1).astype(F32)

    h = x.reshape(t, d)
    p_bf = p.reshape(depth, t, -1).astype(BF16)
    w_in_bf, w_out_e_bf, w_out_o_bf, w_glu_bf = (w.astype(BF16) for w in (w_in_e, w_out_e, w_out_o, w_glu))
    w_up_bf, w_down_bf, w_gate_bf, w_proj_bf = (w.astype(BF16) for w in (w_up, w_down, w_ple_gate, w_ple_proj))
    out = None
    for i in range(depth):
        j = i // 2
        if i % 2 == 0:
            w_small = jnp.pad(w_in_e[j, :, main_cols:], ((0, 0), (0, hd - 2 * heads))).astype(BF16)
            proj, small = _in_proj(h, row(norm_mix[i]), w_in_bf, j, main_cols, w_small)
            proj = proj.reshape(bsz, seq, main_cols)
            small = small.reshape(bsz, seq, hd)
            o_a = _hgrn(proj, lower_bounds[i].reshape(heads, 1, hd), row(g_norm_a[j]),
                        bsz=bsz, seq=seq)
            cw = conv_w[j].reshape(CONV_WIDTH, 3, heads, hd).transpose(2, 1, 0, 3)
            scal = jnp.broadcast_to(jnp.stack([a_log[j], dt_bias[j]], axis=1)[:, :, None],
                                    (heads, 2, hd)).astype(F32)
            o_b = _delta(proj, small, cw, scal, row(g_norm_b[j]), bsz=bsz, seq=seq, col0=4 * heads)
            mix_in = ([o_a.reshape(t, width), o_b.reshape(t, width)], w_out_e_bf, j)
            mix_seq = None
        else:
            act = _s5_act(normed.reshape(t, d), s5_a_re[j], s5_a_im[j], s5_b_re[j], s5_b_im[j],
                          s5_c_re[j], s5_c_im[j], s5_d[j], s5_log_dt[j], bsz=bsz)
            glu = _glu(act, w_glu_bf, j, row(b_glu[j]))
            mix_in = ([glu.reshape(seq // S5_L, bsz, S5_L, d)], w_out_o_bf, j)
            mix_seq = seq
        h = _mix_mlp(*mix_in, h, row(norm_mlp[i]), w_up_bf, w_down_bf, i, chunk_major_seq=mix_seq)
        last = i == depth - 1
        feeds_s5 = not last and (i + 1) % 2 == 1
        h, normed = _ple(h, row(norm_ple[i]), w_gate_bf, p_bf, w_proj_bf,
                         row(final_norm if last else norm_mix[i + 1]), i,
                         chunk_major=(bsz, seq) if feeds_s5 else None)
        out = normed
    return out.reshape(bsz, seq, d)
```

```python
import functools
import math

import jax
import jax.numpy as jnp
from jax import lax
from jax.experimental import pallas as pl
from jax.experimental.pallas import tpu as pltpu

F32 = jnp.float32
BF16 = jnp.bfloat16

SUBLANES = 8
NORM_EPS = 1e-6
CHUNK = 64
HEAD_DIM = 128
N_HEADS = 8
CONV_WIDTH = 4
S5_GROUP = 16
S5_STATE = 64
S5_L = 16
S5_W = S5_L * S5_GROUP
S5_PACK = 128 // S5_GROUP
EXP_CLAMP = 80.0

VMEM_LIMIT = 56 * 1024 * 1024


def _sigmoid(x):
    return 0.5 * jnp.tanh(0.5 * x) + 0.5


def _silu(x):
    return x * _sigmoid(x)


def _rms(x, g):
    return x * lax.rsqrt(jnp.mean(x * x, axis=-1, keepdims=True) + NORM_EPS) * g


def _dot(a, b):
    return jnp.dot(a, b, preferred_element_type=F32)


def _dot_nt(a, b):
    return lax.dot_general(a, b, (((1,), (1,)), ((), ())), preferred_element_type=F32)


def _dot_tn(a, b):
    return lax.dot_general(a, b, (((0,), (0,)), ((), ())), preferred_element_type=F32)


def _dot_split(a_bf, x):
    hi = x.astype(BF16)
    lo = (x - hi.astype(F32)).astype(BF16)
    return _dot(a_bf, hi) + _dot(a_bf, lo)


def _params(*sem):
    return pltpu.CompilerParams(dimension_semantics=sem, vmem_limit_bytes=VMEM_LIMIT)


def _in_proj_kernel(x_ref, g_ref, w_ref, ws_ref, o_ref, os_ref, hn_ref):
    @pl.when(pl.program_id(1) == 0)
    def _():
        hn = _rms(x_ref[...], g_ref[...]).astype(BF16)
        hn_ref[...] = hn
        os_ref[...] = _dot(hn, ws_ref[...])

    o_ref[...] = _dot(hn_ref[...], w_ref[...])


def _in_proj(x, g, w, layer, n, w_small, *, tm=1024, tn=512):
    t, d = x.shape
    tm = min(tm, t)
    ns = w_small.shape[1]
    return pl.pallas_call(
        _in_proj_kernel,
        out_shape=(jax.ShapeDtypeStruct((t, n), F32), jax.ShapeDtypeStruct((t, ns), F32)),
        grid=(t // tm, n // tn),
        in_specs=[pl.BlockSpec((tm, d), lambda i, j: (i, 0)),
                  pl.BlockSpec((1, d), lambda i, j: (0, 0)),
                  pl.BlockSpec((None, d, tn), lambda i, j: (layer, 0, j)),
                  pl.BlockSpec((d, ns), lambda i, j: (0, 0))],
        out_specs=(pl.BlockSpec((tm, tn), lambda i, j: (i, j)),
                   pl.BlockSpec((tm, ns), lambda i, j: (i, 0))),
        scratch_shapes=[pltpu.VMEM((tm, d), BF16)],
        compiler_params=_params("parallel", "arbitrary"),
        name="in_proj",
    )(x, g, w, w_small)


def _chunk_major_spec(tm, width, seq, **kw):
    per_seq = seq // tm
    return pl.BlockSpec((tm // S5_L, None, S5_L, width),
                        lambda i, *_: (i % per_seq, i // per_seq, 0, 0), **kw)


def _resident(shape, layer=None, block=0):
    if layer is None:
        return pl.BlockSpec(shape, lambda *_: (0,) * len(shape), pipeline_mode=pl.Buffered(1))
    return pl.BlockSpec((None,) + tuple(shape), lambda *_: (layer, block, 0),
                        pipeline_mode=pl.Buffered(1))


def _mix_mlp_kernel(*refs, n_in):
    x_refs, w_refs = refs[:n_in], refs[n_in:2 * n_in]
    r_ref, g_ref, wu_ref, wd_ref, o_ref, hn_ref = refs[2 * n_in:]

    @pl.when(pl.program_id(1) == 0)
    def _():
        h = r_ref[...]
        for x_ref, w_ref in zip(x_refs, w_refs):
            h = h + _dot(x_ref[...].reshape(-1, x_ref.shape[-1]), w_ref[...])
        hn_ref[...] = _rms(h, g_ref[...]).astype(BF16)
        o_ref[...] = h

    a = jnp.maximum(_dot(hn_ref[...], wu_ref[...]), 0.0)
    o_ref[...] += _dot((a * a).astype(BF16), wd_ref[...])


def _mix_mlp(xs, w_mix, mix_layer, resid, g, w_up, w_down, layer, *, chunk_major_seq=None,
             tm=512, tf=1024):
    t, d = resid.shape
    f = w_up.shape[2]
    n_in = len(xs)
    kx = w_mix.shape[1] // n_in
    if chunk_major_seq is None:
        x_specs = [pl.BlockSpec((tm, x.shape[1]), lambda i, j: (i, 0)) for x in xs]
    else:
        x_specs = [_chunk_major_spec(tm, x.shape[-1], chunk_major_seq) for x in xs]
    return pl.pallas_call(
        functools.partial(_mix_mlp_kernel, n_in=n_in),
        out_shape=jax.ShapeDtypeStruct((t, d), F32),
        grid=(t // tm, f // tf),
        in_specs=(x_specs + [_resident((kx, d), mix_layer, k) for k in range(n_in)]
                  + [pl.BlockSpec((tm, d), lambda i, j: (i, 0)),
                     pl.BlockSpec((1, d), lambda i, j: (0, 0)),
                     pl.BlockSpec((None, d, tf), lambda i, j: (layer, 0, j)),
                     pl.BlockSpec((None, tf, d), lambda i, j: (layer, j, 0))]),
        out_specs=pl.BlockSpec((tm, d), lambda i, j: (i, 0)),
        scratch_shapes=[pltpu.VMEM((tm, d), BF16)],
        compiler_params=_params("parallel", "arbitrary"),
        name="mix_mlp",
    )(*xs, *([w_mix] * n_in), resid, g, w_up, w_down)


def _ple_kernel(h_ref, g_ref, wg_ref, p_ref, wp_ref, g2_ref, o_ref, on_ref):
    h = h_ref[...]
    gate = _sigmoid(_dot(_rms(h, g_ref[...]).astype(BF16), wg_ref[...]))
    h_new = h + gate * _dot(p_ref[...], wp_ref[...])
    o_ref[...] = h_new
    on_ref[...] = _rms(h_new, g2_ref[...]).reshape(on_ref.shape)


def _ple(h, g, w_gate, p, w_proj, g_next, layer, *, chunk_major=None, tm=512):
    t, d = h.shape
    pd = p.shape[2]
    if chunk_major is None:
        normed_shape, normed_spec = (t, d), pl.BlockSpec((tm, d), lambda i: (i, 0))
    else:
        bsz, seq = chunk_major
        normed_shape, normed_spec = (seq // S5_L, bsz, S5_L, d), _chunk_major_spec(tm, d, seq)
    return pl.pallas_call(
        _ple_kernel,
        out_shape=(jax.ShapeDtypeStruct((t, d), F32), jax.ShapeDtypeStruct(normed_shape, F32)),
        grid=(t // tm,),
        in_specs=[pl.BlockSpec((tm, d), lambda i: (i, 0)),
                  _resident((1, d)),
                  _resident((d, d), layer),
                  pl.BlockSpec((None, tm, pd), lambda i: (layer, i, 0)),
                  _resident((pd, d), layer),
                  _resident((1, d))],
        out_specs=(pl.BlockSpec((tm, d), lambda i: (i, 0)), normed_spec),
        compiler_params=_params("parallel"),
        name="ple",
    )(h, g, w_gate, p, w_proj, g_next)


def _glu_kernel(a_ref, w_ref, b_ref, o_ref):
    a = a_ref[...]
    z = _dot(a.astype(BF16), w_ref[...]) + b_ref[...]
    o_ref[...] = (a * _sigmoid(z)).astype(BF16)


def _glu(act, w, layer, b, *, tm=512):
    t, d = act.shape
    return pl.pallas_call(
        _glu_kernel,
        out_shape=jax.ShapeDtypeStruct((t, d), BF16),
        grid=(t // tm,),
        in_specs=[pl.BlockSpec((tm, d), lambda i: (i, 0)), _resident((d, d), layer),
                  _resident((1, d))],
        out_specs=pl.BlockSpec((tm, d), lambda i: (i, 0)),
        compiler_params=_params("parallel"),
        name="glu",
    )(act, w, b)


def _row(x, t):
    return x[t:t + 1, :]


def _run_lockstep(gens):
    live = list(gens)
    while live:
        nxt = []
        for g in live:
            try:
                next(g)
                nxt.append(g)
            except StopIteration:
                pass
        live = nxt


def _hgrn_kernel(q_ref, f_ref, i_ref, g_ref, lb_ref, gn_ref, o_ref, st_ref, sc_ref, ost_ref, vb_ref,
                 *, n_chunks, hb):
    c_len, d = CHUNK, HEAD_DIM

    @pl.when(pl.program_id(2) == 0)
    def _():
        st_ref[...] = jnp.zeros_like(st_ref)

    ri = lax.broadcasted_iota(jnp.int32, (c_len, d), 0)
    rt = lax.broadcasted_iota(jnp.int32, (c_len, c_len), 0)
    rs = lax.broadcasted_iota(jnp.int32, (c_len, c_len), 1)
    tril = (rs <= rt).astype(BF16)
    diag_mask = jnp.logical_and(rs <= rt, (rs // 16) == (rt // 16))
    gn = gn_ref[...]

    def phase_a(hh, r0, slot):
        rows = pl.ds(r0, c_len)
        cols = slice(hh * d, (hh + 1) * d)
        lb = lb_ref[hh]
        q = q_ref[rows, cols]
        forget = lb + (1.0 - lb) * _sigmoid(f_ref[rows, cols])
        k = 1.0 - forget
        v = i_ref[rows, cols].astype(BF16)
        cum = _dot_split(tril, jnp.log(forget))
        yield

        def side(valid, ref_row, sign, x):
            e = jnp.where(valid, sign * (cum - ref_row), 0.0)
            return jnp.where(valid, x * jnp.exp(e), 0.0)

        c31 = _row(cum, 31)
        ref_b = jnp.where(ri < 32, _row(cum, 15), _row(cum, 47))
        ref_d = jnp.where(ri < 16, _row(cum, 8),
                          jnp.where(ri < 32, _row(cum, 24),
                                    jnp.where(ri < 48, _row(cum, 40), _row(cum, 56))))
        hi16 = (ri % 32) >= 16
        q_b = side(hi16, ref_b, 1.0, q)
        k_b = side(jnp.logical_not(hi16), ref_b, -1.0, k)
        q_off = jnp.concatenate([side(ri >= 32, c31, 1.0, q),
                                 jnp.where(ri < 32, q_b, 0.0),
                                 jnp.where(ri >= 32, q_b, 0.0)], axis=1).astype(BF16)
        k_off = jnp.concatenate([side(ri < 32, c31, -1.0, k),
                                 jnp.where(ri < 32, k_b, 0.0),
                                 jnp.where(ri >= 32, k_b, 0.0)], axis=1).astype(BF16)
        q_d = (q * jnp.exp(jnp.minimum(cum - ref_d, EXP_CLAMP))).astype(BF16)
        k_d = (k * jnp.exp(jnp.minimum(ref_d - cum, EXP_CLAMP))).astype(BF16)
        s_off = _dot_nt(q_off, k_off)
        s_diag = _dot_nt(q_d, k_d)
        cum_end = _row(cum, c_len - 1)
        q_dec = (q * jnp.exp(cum)).astype(BF16)
        k_dec = (k * jnp.exp(cum_end - cum)).astype(BF16)
        st = st_ref[hh]
        out_st = _dot_nt(q_dec, st.astype(BF16))
        st_new = st * jnp.exp(cum_end) + _dot_tn(v, k_dec)
        yield
        st_ref[hh] = st_new
        sc_ref[slot, hh] = (s_off + jnp.where(diag_mask, s_diag, 0.0)).astype(BF16)
        ost_ref[slot, hh] = out_st
        vb_ref[slot, hh] = v

    def phase_b(hh, r0, slot):
        rows = pl.ds(r0, c_len)
        cols = slice(hh * d, (hh + 1) * d)
        out = _dot(sc_ref[slot, hh], vb_ref[slot, hh]) + ost_ref[slot, hh]
        yield
        out = _rms(out, gn) * _silu(g_ref[rows, cols])
        o_ref[rows, cols] = out.astype(o_ref.dtype)

    def first_chunk(c, carry):
        _run_lockstep([phase_a(hh, pl.multiple_of(c * c_len, c_len), c % 2) for hh in range(hb)])
        return carry

    def chunk(c, carry):
        r0 = pl.multiple_of(c * c_len, c_len)
        slot = c % 2
        _run_lockstep([phase_b(hh, pl.multiple_of(r0 - c_len, c_len), 1 - slot) for hh in range(hb)]
                      + [phase_a(hh, r0, slot) for hh in range(hb)])
        return carry

    lax.fori_loop(0, 1, first_chunk, 0)
    lax.fori_loop(1, n_chunks, chunk, 0)
    _run_lockstep([phase_b(hh, (n_chunks - 1) * c_len, (n_chunks - 1) % 2) for hh in range(hb)])


def _hgrn(proj, lb, g_norm, *, bsz, seq, sb=512, hb=8):
    sb = min(sb, seq)
    h = N_HEADS
    wb = hb * HEAD_DIM

    def spec(off):
        return pl.BlockSpec((None, sb, wb), lambda b, hh, s: (b, s, hh + off // hb))

    return pl.pallas_call(
        functools.partial(_hgrn_kernel, n_chunks=sb // CHUNK, hb=hb),
        out_shape=jax.ShapeDtypeStruct((bsz, seq, h * HEAD_DIM), BF16),
        grid=(bsz, h // hb, seq // sb),
        in_specs=[spec(0), spec(h), spec(2 * h), spec(3 * h),
                  pl.BlockSpec((hb, 1, HEAD_DIM), lambda b, hh, s: (hh, 0, 0)),
                  pl.BlockSpec((1, HEAD_DIM), lambda b, hh, s: (0, 0))],
        out_specs=pl.BlockSpec((None, sb, wb), lambda b, hh, s: (b, s, hh)),
        scratch_shapes=[pltpu.VMEM((hb, HEAD_DIM, HEAD_DIM), F32),
                        pltpu.VMEM((2, hb, CHUNK, CHUNK), BF16),
                        pltpu.VMEM((2, hb, CHUNK, HEAD_DIM), F32),
                        pltpu.VMEM((2, hb, CHUNK, HEAD_DIM), BF16)],
        compiler_params=_params("parallel", "parallel", "arbitrary"),
        name="hgrn2",
    )(proj, proj, proj, proj, lb, g_norm)


def _delta_kernel(q_ref, k_ref, v_ref, z_ref, ab_ref, cw_ref, sc_ref, gn_ref, o_ref,
                  st_ref, hist_ref, win_ref, td_ref, nm_ref, rhs_ref, in_ref, qd_ref, kd_ref, dec_ref,
                  *, n_chunks, hb):
    c_len, d = CHUNK, HEAD_DIM
    head0 = pl.program_id(1) * hb

    @pl.when(pl.program_id(2) == 0)
    def _():
        st_ref[...] = jnp.zeros_like(st_ref)
        hist_ref[...] = jnp.zeros_like(hist_ref)

    rt = lax.broadcasted_iota(jnp.int32, (c_len, c_len), 0)
    rs = lax.broadcasted_iota(jnp.int32, (c_len, c_len), 1)
    causal = rs <= rt
    strict = rs < rt
    tril = causal.astype(BF16)
    ones_cc = jnp.ones((c_len, c_len), BF16)
    eye = (rs == rt).astype(F32)
    bt, bs = rt // 16, rs // 16
    m_diag = jnp.logical_and(strict, bt == bs)
    m_l1 = jnp.logical_and(bt // 2 == bs // 2, bt == bs + 1)
    m_l2 = jnp.logical_and(bt >= 2, bs < 2)
    lane = lax.broadcasted_iota(jnp.int32, (c_len, d), 1)
    gn = gn_ref[...]

    def conv(ref, which, hh, c, r0):
        cols = slice(hh * d, (hh + 1) * d)
        cw = cw_ref[hh, which]
        prev0 = pl.multiple_of(jnp.maximum(r0 - 8, 0), 8)
        prev = jnp.where(c > 0, ref[pl.ds(prev0, 8), cols], hist_ref[which, :, cols])
        cur = ref[pl.ds(r0, c_len), cols]
        win_ref[which, hh, 0:8, :] = prev
        win_ref[which, hh, 8:, :] = cur
        acc = cur * cw[CONV_WIDTH - 1:CONV_WIDTH, :]
        for j in range(1, CONV_WIDTH):
            shifted = win_ref[which, hh, 8 - j:8 - j + c_len, :]
            acc = acc + shifted * cw[CONV_WIDTH - 1 - j:CONV_WIDTH - j, :]
        return _silu(acc)

    def mm(a, b):
        return _dot(a.astype(BF16), b.astype(BF16))

    def chunk_gates(r0):
        ab = ab_ref[pl.ds(r0, c_len), :]
        x = ab + sc_ref[1:2, :]
        softplus = jnp.maximum(x, 0.0) + jnp.log(1.0 + jnp.exp(-jnp.abs(x)))
        return -jnp.exp(sc_ref[0:1, :]) * softplus, _sigmoid(ab)

    def phase_a(hh, c, r0, slot, gates):
        cols = slice(hh * d, (hh + 1) * d)
        head = head0 + hh
        log_a_all, beta_all = gates
        q = conv(q_ref, 0, hh, c, r0)
        k = conv(k_ref, 1, hh, c, r0)
        v = conv(v_ref, 2, hh, c, r0)
        q = q * lax.rsqrt(jnp.sum(q * q, axis=-1, keepdims=True) + NORM_EPS) * (d ** -0.5)
        k = k * lax.rsqrt(jnp.sum(k * k, axis=-1, keepdims=True) + NORM_EPS)
        yield

        la_col = jnp.sum(jnp.where(lane == head, log_a_all, 0.0), axis=-1, keepdims=True)
        beta = jnp.sum(jnp.where(lane == head + N_HEADS, beta_all, 0.0), axis=-1, keepdims=True)
        log_a = jnp.broadcast_to(la_col, (c_len, d))
        cum = _dot_split(tril, log_a)
        cum_row = _dot_split(ones_cc, jnp.where(rs >= rt, log_a[:, :c_len], 0.0))
        k_bf = k.astype(BF16)
        k_beta = k * beta
        kk = _dot_nt(k_beta.astype(BF16), k_bf)
        qk = _dot_nt(q.astype(BF16), k_bf)
        yield
        decay = jnp.exp(jnp.where(causal, cum[:, :c_len] - cum_row, 0.0))
        n_mat = jnp.where(strict, kk * decay, 0.0)

        n_d = jnp.where(m_diag, n_mat, 0.0)
        p2 = mm(n_d, n_d)
        e_cum = jnp.exp(cum)
        cum_end = _row(cum, c_len - 1)
        rhs_ref[slot, hh] = jnp.concatenate([v * beta, k_beta * e_cum], axis=1).astype(BF16)
        qd_ref[slot, hh] = (q * e_cum).astype(BF16)
        kd_ref[slot, hh] = (k * jnp.exp(cum_end - cum)).astype(BF16)
        dec_ref[slot, hh] = jnp.exp(cum_end)
        in_ref[slot, hh] = jnp.where(causal, qk * decay, 0.0).astype(BF16)
        nm_ref[slot, hh] = n_mat
        yield
        p4 = mm(p2, p2)
        a12 = mm(eye - n_d, eye + p2)
        yield
        p8 = mm(p4, p4)
        yield
        a48 = mm(eye + p4, eye + p8)
        yield
        td_ref[slot, hh] = mm(a12, a48)

    def phase_b(hh, r0, slot):
        rows = pl.ds(r0, c_len)
        cols = slice(hh * d, (hh + 1) * d)
        t_d = td_ref[slot, hh]
        n_mat = nm_ref[slot, hh]
        x1 = mm(t_d, jnp.where(m_l1, n_mat, 0.0))
        yield
        t_32 = t_d - mm(x1, t_d)
        yield
        x2 = mm(t_32, jnp.where(m_l2, n_mat, 0.0))
        yield
        t_inv = t_32 - mm(x2, t_32)
        yield
        sol = _dot(t_inv.astype(BF16), rhs_ref[slot, hh])
        yield
        u, w = sol[:, :d], sol[:, d:]
        st = st_ref[hh]
        st_bf = st.astype(BF16)
        v_new = u - _dot(w.astype(BF16), st_bf)
        out_st = _dot(qd_ref[slot, hh], st_bf)
        yield
        v_new_bf = v_new.astype(BF16)
        out = out_st + _dot(in_ref[slot, hh], v_new_bf)
        st_ref[hh] = st * dec_ref[slot, hh] + _dot_tn(kd_ref[slot, hh], v_new_bf)
        yield
        out = _rms(out, gn) * _silu(z_ref[rows, cols])
        o_ref[rows, cols] = out.astype(o_ref.dtype)

    def first_chunk(c, carry):
        r0 = pl.multiple_of(c * c_len, c_len)
        gates = chunk_gates(r0)
        _run_lockstep([phase_a(hh, c, r0, c % 2, gates) for hh in range(hb)])
        return carry

    def chunk(c, carry):
        r0 = pl.multiple_of(c * c_len, c_len)
        slot = c % 2
        gates = chunk_gates(r0)
        _run_lockstep([phase_b(hh, pl.multiple_of(r0 - c_len, c_len), 1 - slot) for hh in range(hb)]
                      + [phase_a(hh, c, r0, slot, gates) for hh in range(hb)])
        return carry

    lax.fori_loop(0, 1, first_chunk, 0)
    lax.fori_loop(1, n_chunks, chunk, 0)
    _run_lockstep([phase_b(hh, (n_chunks - 1) * c_len, (n_chunks - 1) % 2) for hh in range(hb)])
    last8 = pl.ds(n_chunks * c_len - 8, 8)
    hist_ref[0] = q_ref[last8, :]
    hist_ref[1] = k_ref[last8, :]
    hist_ref[2] = v_ref[last8, :]


def _delta(proj, small, conv_w, scal, g_norm, *, bsz, seq, col0, sb=512, hb=8):
    sb = min(sb, seq)
    h = N_HEADS
    wb = hb * HEAD_DIM

    def spec(off):
        return pl.BlockSpec((None, sb, wb), lambda b, hh, s: (b, s, hh + off // hb))

    return pl.pallas_call(
        functools.partial(_delta_kernel, n_chunks=sb // CHUNK, hb=hb),
        out_shape=jax.ShapeDtypeStruct((bsz, seq, h * HEAD_DIM), BF16),
        grid=(bsz, h // hb, seq // sb),
        in_specs=[spec(col0), spec(col0 + h), spec(col0 + 2 * h), spec(col0 + 3 * h),
                  pl.BlockSpec((None, sb, HEAD_DIM), lambda b, hh, s: (b, s, 0)),
                  pl.BlockSpec((hb, 3, CONV_WIDTH, HEAD_DIM), lambda b, hh, s: (hh, 0, 0, 0)),
                  pl.BlockSpec((2, HEAD_DIM), lambda b, hh, s: (0, 0)),
                  pl.BlockSpec((1, HEAD_DIM), lambda b, hh, s: (0, 0))],
        out_specs=pl.BlockSpec((None, sb, wb), lambda b, hh, s: (b, s, hh)),
        scratch_shapes=[pltpu.VMEM((hb, HEAD_DIM, HEAD_DIM), F32),
                        pltpu.VMEM((3, 8, wb), F32),
                        pltpu.VMEM((3, hb, 8 + CHUNK, HEAD_DIM), F32),
                        pltpu.VMEM((2, hb, CHUNK, CHUNK), F32),
                        pltpu.VMEM((2, hb, CHUNK, CHUNK), F32),
                        pltpu.VMEM((2, hb, CHUNK, 2 * HEAD_DIM), BF16),
                        pltpu.VMEM((2, hb, CHUNK, CHUNK), BF16),
                        pltpu.VMEM((2, hb, CHUNK, HEAD_DIM), BF16),
                        pltpu.VMEM((2, hb, CHUNK, HEAD_DIM), BF16),
                        pltpu.VMEM((2, hb, 1, HEAD_DIM), F32)],
        compiler_params=_params("parallel", "parallel", "arbitrary"),
        name="deltanet",
    )(proj, proj, proj, proj, small, conv_w, scal, g_norm)


def _cmul(ar, ai, br, bi):
    return ar * br - ai * bi, ar * bi + ai * br


def _s5_expand(a, g8):
    gs, lanes = S5_GROUP, S5_PACK * S5_GROUP
    sel = (lax.broadcasted_iota(jnp.int32, (a.shape[0], lanes), 1) // gs) == g8
    pieces = []
    for k in range(S5_L):
        src = a[:, (k // S5_PACK) * lanes:(k // S5_PACK + 1) * lanes]
        shift = ((g8 - k % S5_PACK + S5_PACK) * gs) % lanes
        pieces.append(jnp.where(sel, pltpu.roll(src, shift, axis=1), 0.0))
    return jnp.concatenate(pieces, axis=1)


def _s5_prep_kernel(are_ref, aim_ref, ldt_ref, bre_ref, bim_ref, cre_ref, cim_ref,
                    are2_ref, aim2_ref, ldt2_ref, w8_ref, m1_ref, m2_ref, lam_ref):
    def one_group(g8, carry):
        _s5_prep_group(g8, are_ref[g8], aim_ref[g8], ldt_ref[g8], bre_ref[g8], bim_ref[g8],
                       cre_ref[g8], cim_ref[g8], are2_ref[g8], aim2_ref[g8], ldt2_ref[g8],
                       w8_ref, m1_ref, m2_ref, lam_ref)
        return carry

    lax.fori_loop(0, S5_PACK, one_group, 0)


def _s5_prep_group(g8, are, aim, ldt, bre, bim, cre, cim, are2, aim2, ldt2,
                   w8_ref, m1_ref, m2_ref, lam_ref):
    p, w, l, gs = S5_STATE, S5_W, S5_L, S5_GROUP
    lanes = S5_PACK * gs
    hp = lax.Precision.HIGHEST
    dt = jnp.exp(ldt)
    a_re = jnp.broadcast_to(are, (p, w))
    a_im = jnp.broadcast_to(aim, (p, w))

    e_lane = jnp.minimum(lax.broadcasted_iota(jnp.int32, (p, lanes), 1), l).astype(F32)
    lr, li = e_lane * (are * dt), e_lane * (aim * dt)
    mag = jnp.exp(lr)
    powers = jnp.concatenate([mag * jnp.cos(li), mag * jnp.sin(li)], axis=0)
    sel_e = lax.broadcasted_iota(jnp.int32, (lanes, w), 0)
    sel_k = lax.broadcasted_iota(jnp.int32, (lanes, w), 1) // gs
    exps = (lambda k: k, lambda k: l - 1 - k, lambda k: k + 1)
    sel = jnp.concatenate([(sel_e == f(sel_k)).astype(BF16) for f in exps], axis=1)
    tiled, rest = 0.0, powers
    for _ in range(3):
        part = rest.astype(BF16)
        tiled = tiled + _dot(part, sel)
        rest = rest - part.astype(F32)

    def lam_pow(i):
        return tiled[:p, i * w:(i + 1) * w], tiled[p:, i * w:(i + 1) * w]

    lb_re = jnp.broadcast_to(powers[:p, 1:2], (p, w))
    lb_im = jnp.broadcast_to(powers[p:, 1:2], (p, w))
    den = a_re * a_re + a_im * a_im
    xr, xi = lb_re - 1.0, lb_im
    coef_re, coef_im = (xr * a_re + xi * a_im) / den, (xi * a_re - xr * a_im) / den
    bb_re, bb_im = _cmul(coef_re, coef_im, bre, bim)

    e_re, e_im = _cmul(*lam_pow(0), cre, cim)
    lhs = jnp.concatenate([bb_re[:, :gs], -bb_im[:, :gs]], axis=0)
    rhs = jnp.concatenate([e_re, e_im], axis=0)
    r0 = lax.dot_general(lhs, rhs, (((0,), (0,)), ((), ())), precision=hp,
                         preferred_element_type=F32)
    r0x = _s5_expand(r0, g8).astype(w8_ref.dtype)
    for s in range(l):
        rows = pl.ds(pl.multiple_of(s * lanes + g8 * gs, gs), gs)
        if s:
            w8_ref[rows, :s * lanes] = jnp.zeros((gs, s * lanes), w8_ref.dtype)
        w8_ref[rows, s * lanes:] = r0x[:, :(l - s) * lanes]

    st_rows = pl.ds(pl.multiple_of(g8 * 2 * p, 2 * p), 2 * p)
    d_re, d_im = _cmul(*lam_pow(1), bb_re, bb_im)
    m1_ref[st_rows, :] = _s5_expand(jnp.concatenate([d_re, d_im], axis=0), g8).astype(m1_ref.dtype)
    f_re, f_im = _cmul(*lam_pow(2), cre, cim)
    m2_ref[st_rows, :] = _s5_expand(jnp.concatenate([f_re, -f_im], axis=0), g8).astype(m2_ref.dtype)

    dt2 = jnp.exp(ldt2)
    mag_l = jnp.exp(float(l) * are2 * dt2)
    ang = float(l) * aim2 * dt2
    ll_re, ll_im = mag_l * jnp.cos(ang), mag_l * jnp.sin(ang)
    first = lax.broadcasted_iota(jnp.int32, ll_im.shape, 1) < p
    lam_ref[g8] = jnp.concatenate([ll_re, jnp.where(first, -ll_im, ll_im)]
                                  + [jnp.zeros_like(ll_re)] * 6, axis=0)


def _s5_prep(a_re, a_im, log_dt, b_re, b_im, c_re, c_im):
    g, p = a_re.shape
    w, l, pk = S5_W, S5_L, S5_PACK
    wx = l * pk * S5_GROUP
    col = lambda x: x.reshape(g, p, 1)
    tile = lambda x: jnp.tile(x, (1, 1, l))
    dup = lambda x: jnp.concatenate([x, x], axis=-1).reshape(g, 1, 2 * p)
    ldt2 = jnp.broadcast_to(log_dt.reshape(g, 1, 1), (g, 1, 2 * p))
    args = (col(a_re), col(a_im), log_dt.reshape(g, 1, 1), tile(b_re), tile(b_im),
            tile(jnp.swapaxes(c_re, 1, 2)), tile(jnp.swapaxes(c_im, 1, 2)),
            dup(a_re), dup(a_im), ldt2)

    def gspec(shape):
        return pl.BlockSpec((pk,) + shape, lambda i: (i,) + (0,) * len(shape))

    def ospec(shape):
        return pl.BlockSpec((None,) + shape, lambda i: (i,) + (0,) * len(shape))

    return pl.pallas_call(
        _s5_prep_kernel,
        out_shape=(jax.ShapeDtypeStruct((g // pk, wx, wx), BF16),
                   jax.ShapeDtypeStruct((g // pk, pk * 2 * p, wx), BF16),
                   jax.ShapeDtypeStruct((g // pk, pk * 2 * p, wx), BF16),
                   jax.ShapeDtypeStruct((g, 8, 2 * p), F32)),
        grid=(g // pk,),
        in_specs=[gspec((p, 1)), gspec((p, 1)), gspec((1, 1)), gspec((p, w)), gspec((p, w)),
                  gspec((p, w)), gspec((p, w)), gspec((1, 2 * p)), gspec((1, 2 * p)),
                  gspec((1, 2 * p))],
        out_specs=(ospec((wx, wx)), ospec((pk * 2 * p, wx)), ospec((pk * 2 * p, wx)),
                   gspec((8, 2 * p))),
        compiler_params=_params("parallel"),
        name="s5_prep",
    )(*args)


def _s5_gather(u_ref, x8_ref, nbk):
    lanes = u_ref.shape[1]
    for t in range(S5_L):
        x8_ref[:, t * lanes:(t + 1) * lanes] = u_ref[pl.ds(t, nbk, stride=S5_L), :].astype(BF16)


def _s5_inc_kernel(u_ref, m1_ref, inc_ref, x8_ref):
    _s5_gather(u_ref, x8_ref, inc_ref.shape[0])
    inc_ref[...] = _dot_nt(x8_ref[...], m1_ref[...])


def _s5_inc(u, m1, *, rb=2):
    t, d = u.shape
    ngb, sp, wx = m1.shape
    lanes = d // ngb
    rows = t // rb
    nbk = rows // S5_L
    return pl.pallas_call(
        _s5_inc_kernel,
        out_shape=jax.ShapeDtypeStruct((t // S5_L, ngb * sp), F32),
        grid=(ngb, rb),
        in_specs=[pl.BlockSpec((rows, lanes), lambda i, r: (r, i)),
                  pl.BlockSpec((None, sp, wx), lambda i, r: (i, 0, 0))],
        out_specs=pl.BlockSpec((nbk, sp), lambda i, r: (r, i)),
        scratch_shapes=[pltpu.VMEM((nbk, wx), BF16)],
        compiler_params=_params("parallel", "arbitrary"),
        name="s5_inc",
    )(u, m1)


def _s5_scan_kernel(inc_ref, lam_ref, x_ref, *, bsz, n_steps):
    lam = lam_ref[...]
    a, bc = lam[0:1, :], lam[1:2, :]
    width = inc_ref.shape[1]
    sp = 2 * S5_STATE
    per = SUBLANES // bsz
    first = (lax.broadcasted_iota(jnp.int32, (SUBLANES, width), 1) % sp) < S5_STATE
    row = lax.broadcasted_iota(jnp.int32, (SUBLANES, width), 0)

    def swap(v):
        return jnp.where(first, pltpu.roll(v, width - S5_STATE, axis=1),
                         pltpu.roll(v, S5_STATE, axis=1))

    def tile_step(m, carry):
        x, xs = carry
        rows = pl.ds(pl.multiple_of(m * SUBLANES, SUBLANES), SUBLANES)
        inc_tile = inc_ref[rows, :]
        out = x
        for j in range(per):
            inc = inc_tile if j == 0 else pltpu.roll(inc_tile, SUBLANES - j * bsz, axis=0)
            x, xs = a * x + bc * xs + inc, a * xs - bc * x + swap(inc)
            if j + 1 < per:
                out = jnp.where(row < (j + 1) * bsz, out, pltpu.roll(x, (j + 1) * bsz, axis=0))
        x_ref[rows, :] = out
        return x, xs

    zero = jnp.zeros((SUBLANES, width), F32)
    lax.fori_loop(0, n_steps // per, tile_step, (zero, zero), unroll=4)


def _s5_scan(inc, lam_rows, *, bsz, wb=2048):
    n, width = inc.shape
    blk = pl.BlockSpec((n, wb), lambda i: (0, i))
    return pl.pallas_call(
        functools.partial(_s5_scan_kernel, bsz=bsz, n_steps=n // bsz),
        out_shape=jax.ShapeDtypeStruct((n, width), F32),
        grid=(width // wb,),
        in_specs=[blk, pl.BlockSpec((8, wb), lambda i: (0, i))],
        out_specs=blk,
        compiler_params=_params("parallel"),
        name="s5_scan",
    )(inc, lam_rows)


def _s5_out_kernel(u_ref, x_ref, w8_ref, m2_ref, d_ref, o_ref, x8_ref):
    nbk = x_ref.shape[0]
    lanes = u_ref.shape[1]
    _s5_gather(u_ref, x8_ref, nbk)
    y8 = _dot(x8_ref[...], w8_ref[...]) + _dot(x_ref[...].astype(BF16), m2_ref[...])
    c0 = math.sqrt(2.0 / math.pi)
    d_skip = d_ref[...]
    for t in range(S5_L):
        rows = pl.ds(t, nbk, stride=S5_L)
        y = y8[:, t * lanes:(t + 1) * lanes] + d_skip * u_ref[rows, :]
        o_ref[rows, :] = 0.5 * y * (1.0 + jnp.tanh(c0 * (y + 0.044715 * (y * y * y))))


def _s5_out(u, x_all, w8, m2, d_skip, *, rb=2):
    t, d = u.shape
    ngb, sp, wx = m2.shape
    lanes = d // ngb
    rows = t // rb
    nbk = rows // S5_L
    return pl.pallas_call(
        _s5_out_kernel,
        out_shape=jax.ShapeDtypeStruct((t, d), F32),
        grid=(ngb, rb),
        in_specs=[pl.BlockSpec((rows, lanes), lambda i, r: (r, i)),
                  pl.BlockSpec((nbk, sp), lambda i, r: (r, i)),
                  pl.BlockSpec((None, wx, wx), lambda i, r: (i, 0, 0)),
                  pl.BlockSpec((None, sp, wx), lambda i, r: (i, 0, 0)),
                  pl.BlockSpec((1, lanes), lambda i, r: (0, i))],
        out_specs=pl.BlockSpec((rows, lanes), lambda i, r: (r, i)),
        scratch_shapes=[pltpu.VMEM((nbk, wx), BF16)],
        compiler_params=_params("parallel", "arbitrary"),
        name="s5_out",
    )(u, x_all, w8, m2, d_skip)


def _s5_act(u, a_re, a_im, b_re, b_im, c_re, c_im, d_skip, log_dt, *, bsz):
    d = u.shape[1]
    g = d // S5_GROUP
    w8, m1, m2, lam = _s5_prep(a_re, a_im, log_dt, b_re, b_im, c_re, c_im)
    lam_rows = jnp.swapaxes(lam, 0, 1).reshape(8, g * 2 * S5_STATE)
    inc = _s5_inc(u, m1)
    x_all = _s5_scan(inc, lam_rows, bsz=bsz)
    return _s5_out(u, x_all, w8, m2, d_skip.reshape(1, d).astype(F32))


def kernel(x, p, norm_mix, norm_mlp, norm_ple, w_in_e, w_out_e, hgrn_lb, g_norm_a, conv_w, a_log, dt_bias, g_norm_b, s5_a_re, s5_a_im, s5_b_re, s5_b_im, s5_c_re, s5_c_im, s5_d, s5_log_dt, w_glu, b_glu, w_out_o, w_up, w_down, w_ple_gate, w_ple_proj, final_norm):
    bsz, seq, d = x.shape
    t = bsz * seq
    depth = p.shape[0]
    heads, hd = N_HEADS, HEAD_DIM
    width = heads * hd
    main_cols = 8 * width
    lower_bounds = jnp.cumsum(jax.nn.softmax(hgrn_lb.astype(F32), axis=0), axis=0)
    row = lambda v: v.reshape(1, -1).astype(F32)

    h = x.reshape(t, d)
    p_bf = p.reshape(depth, t, -1).astype(BF16)
    w_in_bf, w_out_e_bf, w_out_o_bf, w_glu_bf = (w.astype(BF16) for w in (w_in_e, w_out_e, w_out_o, w_glu))
    w_up_bf, w_down_bf, w_gate_bf, w_proj_bf = (w.astype(BF16) for w in (w_up, w_down, w_ple_gate, w_ple_proj))
    out = None
    for i in range(depth):
        j = i // 2
        if i % 2 == 0:
            w_small = jnp.pad(w_in_e[j, :, main_cols:], ((0, 0), (0, hd - 2 * heads))).astype(BF16)
            proj, small = _in_proj(h, row(norm_mix[i]), w_in_bf, j, main_cols, w_small)
            proj = proj.reshape(bsz, seq, main_cols)
            small = small.reshape(bsz, seq, hd)
            o_a = _hgrn(proj, lower_bounds[i].reshape(heads, 1, hd), row(g_norm_a[j]),
                        bsz=bsz, seq=seq)
            cw = conv_w[j].reshape(CONV_WIDTH, 3, heads, hd).transpose(2, 1, 0, 3)
            scal = jnp.pad(jnp.stack([a_log[j], dt_bias[j]]).astype(F32), ((0, 0), (0, hd - heads)))
            o_b = _delta(proj, small, cw, scal, row(g_norm_b[j]), bsz=bsz, seq=seq, col0=4 * heads)
            mix_in = ([o_a.reshape(t, width), o_b.reshape(t, width)], w_out_e_bf, j)
            mix_seq = None
        else:
            act = _s5_act(normed.reshape(t, d), s5_a_re[j], s5_a_im[j], s5_b_re[j], s5_b_im[j],
                          s5_c_re[j], s5_c_im[j], s5_d[j], s5_log_dt[j], bsz=bsz)
            glu = _glu(act, w_glu_bf, j, row(b_glu[j]))
            mix_in = ([glu.reshape(seq // S5_L, bsz, S5_L, d)], w_out_o_bf, j)
            mix_seq = seq
        h = _mix_mlp(*mix_in, h, row(norm_mlp[i]), w_up_bf, w_down_bf, i, chunk_major_seq=mix_seq)
        last = i == depth - 1
        feeds_s5 = not last and (i + 1) % 2 == 1
        h, normed = _ple(h, row(norm_ple[i]), w_gate_bf, p_bf, w_proj_bf,
                         row(final_norm if last else norm_mix[i + 1]), i,
                         chunk_major=(bsz, seq) if feeds_s5 else None)
        out = normed
    return out.reshape(bsz, seq, d)
```

```python
import functools
import math

import jax
import jax.numpy as jnp
from jax import lax
from jax.experimental import pallas as pl
from jax.experimental.pallas import tpu as pltpu

F32 = jnp.float32
BF16 = jnp.bfloat16

SUBLANES = 8
NORM_EPS = 1e-6
CHUNK = 64
HEAD_DIM = 128
N_HEADS = 8
CONV_WIDTH = 4
S5_GROUP = 16
S5_STATE = 64
S5_L = 16
S5_W = S5_L * S5_GROUP
S5_PACK = 128 // S5_GROUP
EXP_CLAMP = 80.0

VMEM_LIMIT = 56 * 1024 * 1024


def _sigmoid(x):
    return 0.5 * jnp.tanh(0.5 * x) + 0.5


def _silu(x):
    return x * _sigmoid(x)


def _rms(x, g):
    return x * lax.rsqrt(jnp.mean(x * x, axis=-1, keepdims=True) + NORM_EPS) * g


def _dot(a, b):
    return jnp.dot(a, b, preferred_element_type=F32)


def _dot_nt(a, b):
    return lax.dot_general(a, b, (((1,), (1,)), ((), ())), preferred_element_type=F32)


def _dot_tn(a, b):
    return lax.dot_general(a, b, (((0,), (0,)), ((), ())), preferred_element_type=F32)


def _dot_split(a_bf, x):
    hi = x.astype(BF16)
    lo = (x - hi.astype(F32)).astype(BF16)
    return _dot(a_bf, hi) + _dot(a_bf, lo)


def _params(*sem):
    return pltpu.CompilerParams(dimension_semantics=sem, vmem_limit_bytes=VMEM_LIMIT)


def _in_proj_kernel(x_ref, g_ref, w_ref, ws_ref, o_ref, os_ref, hn_ref):
    @pl.when(pl.program_id(1) == 0)
    def _():
        hn = _rms(x_ref[...], g_ref[...]).astype(BF16)
        hn_ref[...] = hn
        os_ref[...] = _dot(hn, ws_ref[...])

    o_ref[...] = _dot(hn_ref[...], w_ref[...])


def _in_proj(x, g, w, layer, n, w_small, *, tm=1024, tn=1024):
    t, d = x.shape
    tm = min(tm, t)
    ns = w_small.shape[1]
    return pl.pallas_call(
        _in_proj_kernel,
        out_shape=(jax.ShapeDtypeStruct((t, n), F32), jax.ShapeDtypeStruct((t, ns), F32)),
        grid=(t // tm, n // tn),
        in_specs=[pl.BlockSpec((tm, d), lambda i, j: (i, 0)),
                  pl.BlockSpec((1, d), lambda i, j: (0, 0)),
                  pl.BlockSpec((None, d, tn), lambda i, j: (layer, 0, j)),
                  pl.BlockSpec((d, ns), lambda i, j: (0, 0))],
        out_specs=(pl.BlockSpec((tm, tn), lambda i, j: (i, j)),
                   pl.BlockSpec((tm, ns), lambda i, j: (i, 0))),
        scratch_shapes=[pltpu.VMEM((tm, d), BF16)],
        compiler_params=_params("parallel", "arbitrary"),
        name="in_proj",
    )(x, g, w, w_small)


def _chunk_major_spec(tm, width, seq, **kw):
    per_seq = seq // tm
    return pl.BlockSpec((tm // S5_L, None, S5_L, width),
                        lambda i, *_: (i % per_seq, i // per_seq, 0, 0), **kw)


def _resident(shape, layer=None, block=0):
    if layer is None:
        return pl.BlockSpec(shape, lambda *_: (0,) * len(shape), pipeline_mode=pl.Buffered(1))
    return pl.BlockSpec((None,) + tuple(shape), lambda *_: (layer, block, 0),
                        pipeline_mode=pl.Buffered(1))


def _mix_mlp_kernel(*refs, n_in):
    x_refs, w_refs = refs[:n_in], refs[n_in:2 * n_in]
    r_ref, g_ref, wu_ref, wd_ref, o_ref, hn_ref = refs[2 * n_in:]

    @pl.when(pl.program_id(1) == 0)
    def _():
        h = r_ref[...]
        for x_ref, w_ref in zip(x_refs, w_refs):
            h = h + _dot(x_ref[...].reshape(-1, x_ref.shape[-1]), w_ref[...])
        hn_ref[...] = _rms(h, g_ref[...]).astype(BF16)
        o_ref[...] = h

    a = jnp.maximum(_dot(hn_ref[...], wu_ref[...]), 0.0)
    o_ref[...] += _dot((a * a).astype(BF16), wd_ref[...])


def _mix_mlp(xs, w_mix, mix_layer, resid, g, w_up, w_down, layer, *, chunk_major_seq=None,
             tm=512, tf=1024):
    t, d = resid.shape
    f = w_up.shape[2]
    n_in = len(xs)
    kx = w_mix.shape[1] // n_in
    if chunk_major_seq is None:
        x_specs = [pl.BlockSpec((tm, x.shape[1]), lambda i, j: (i, 0)) for x in xs]
    else:
        x_specs = [_chunk_major_spec(tm, x.shape[-1], chunk_major_seq) for x in xs]
    return pl.pallas_call(
        functools.partial(_mix_mlp_kernel, n_in=n_in),
        out_shape=jax.ShapeDtypeStruct((t, d), F32),
        grid=(t // tm, f // tf),
        in_specs=(x_specs + [_resident((kx, d), mix_layer, k) for k in range(n_in)]
                  + [pl.BlockSpec((tm, d), lambda i, j: (i, 0)),
                     pl.BlockSpec((1, d), lambda i, j: (0, 0)),
                     pl.BlockSpec((None, d, tf), lambda i, j: (layer, 0, j)),
                     pl.BlockSpec((None, tf, d), lambda i, j: (layer, j, 0))]),
        out_specs=pl.BlockSpec((tm, d), lambda i, j: (i, 0)),
        scratch_shapes=[pltpu.VMEM((tm, d), BF16)],
        compiler_params=_params("parallel", "arbitrary"),
        name="mix_mlp",
    )(*xs, *([w_mix] * n_in), resid, g, w_up, w_down)


def _ple_kernel(h_ref, g_ref, wg_ref, p_ref, wp_ref, g2_ref, o_ref, on_ref):
    h = h_ref[...]
    gate = _sigmoid(_dot(_rms(h, g_ref[...]).astype(BF16), wg_ref[...]))
    h_new = h + gate * _dot(p_ref[...], wp_ref[...])
    o_ref[...] = h_new
    on_ref[...] = _rms(h_new, g2_ref[...]).reshape(on_ref.shape)


def _ple(h, g, w_gate, p, w_proj, g_next, layer, *, chunk_major=None, tm=512):
    t, d = h.shape
    pd = p.shape[2]
    if chunk_major is None:
        normed_shape, normed_spec = (t, d), pl.BlockSpec((tm, d), lambda i: (i, 0))
    else:
        bsz, seq = chunk_major
        normed_shape, normed_spec = (seq // S5_L, bsz, S5_L, d), _chunk_major_spec(tm, d, seq)
    return pl.pallas_call(
        _ple_kernel,
        out_shape=(jax.ShapeDtypeStruct((t, d), F32), jax.ShapeDtypeStruct(normed_shape, F32)),
        grid=(t // tm,),
        in_specs=[pl.BlockSpec((tm, d), lambda i: (i, 0)),
                  _resident((1, d)),
                  _resident((d, d), layer),
                  pl.BlockSpec((None, tm, pd), lambda i: (layer, i, 0)),
                  _resident((pd, d), layer),
                  _resident((1, d))],
        out_specs=(pl.BlockSpec((tm, d), lambda i: (i, 0)), normed_spec),
        compiler_params=_params("parallel"),
        name="ple",
    )(h, g, w_gate, p, w_proj, g_next)


def _glu_kernel(a_ref, w_ref, b_ref, o_ref):
    a = a_ref[...]
    z = _dot(a.astype(BF16), w_ref[...]) + b_ref[...]
    o_ref[...] = (a * _sigmoid(z)).astype(BF16)


def _glu(act, w, layer, b, *, tm=512):
    t, d = act.shape
    return pl.pallas_call(
        _glu_kernel,
        out_shape=jax.ShapeDtypeStruct((t, d), BF16),
        grid=(t // tm,),
        in_specs=[pl.BlockSpec((tm, d), lambda i: (i, 0)), _resident((d, d), layer),
                  _resident((1, d))],
        out_specs=pl.BlockSpec((tm, d), lambda i: (i, 0)),
        compiler_params=_params("parallel"),
        name="glu",
    )(act, w, b)


def _row(x, t):
    return x[t:t + 1, :]


def _run_lockstep(gens):
    live = list(gens)
    while live:
        nxt = []
        for g in live:
            try:
                next(g)
                nxt.append(g)
            except StopIteration:
                pass
        live = nxt


def _hgrn_phases(q_ref, f_ref, i_ref, g_ref, lb_ref, gn_ref, o_ref, st_ref, sc_ref, ost_ref, vb_ref,
                 first_block):
    c_len, d = CHUNK, HEAD_DIM

    @pl.when(first_block)
    def _():
        st_ref[...] = jnp.zeros_like(st_ref)

    ri = lax.broadcasted_iota(jnp.int32, (c_len, d), 0)
    rt = lax.broadcasted_iota(jnp.int32, (c_len, c_len), 0)
    rs = lax.broadcasted_iota(jnp.int32, (c_len, c_len), 1)
    tril = (rs <= rt).astype(BF16)
    diag_mask = jnp.logical_and(rs <= rt, (rs // 16) == (rt // 16))
    gn = gn_ref[...]

    def phase_a(hh, c, r0, slot):
        del c
        rows = pl.ds(r0, c_len)
        cols = slice(hh * d, (hh + 1) * d)
        lb = lb_ref[hh]
        q = q_ref[rows, cols]
        forget = lb + (1.0 - lb) * _sigmoid(f_ref[rows, cols])
        k = 1.0 - forget
        v = i_ref[rows, cols].astype(BF16)
        cum = _dot_split(tril, jnp.log(forget))
        yield

        def side(valid, ref_row, sign, x):
            e = jnp.where(valid, sign * (cum - ref_row), 0.0)
            return jnp.where(valid, x * jnp.exp(e), 0.0)

        c31 = _row(cum, 31)
        ref_b = jnp.where(ri < 32, _row(cum, 15), _row(cum, 47))
        ref_d = jnp.where(ri < 16, _row(cum, 8),
                          jnp.where(ri < 32, _row(cum, 24),
                                    jnp.where(ri < 48, _row(cum, 40), _row(cum, 56))))
        hi16 = (ri % 32) >= 16
        q_b = side(hi16, ref_b, 1.0, q)
        k_b = side(jnp.logical_not(hi16), ref_b, -1.0, k)
        q_off = jnp.concatenate([side(ri >= 32, c31, 1.0, q),
                                 jnp.where(ri < 32, q_b, 0.0),
                                 jnp.where(ri >= 32, q_b, 0.0)], axis=1).astype(BF16)
        k_off = jnp.concatenate([side(ri < 32, c31, -1.0, k),
                                 jnp.where(ri < 32, k_b, 0.0),
                                 jnp.where(ri >= 32, k_b, 0.0)], axis=1).astype(BF16)
        q_d = (q * jnp.exp(jnp.minimum(cum - ref_d, EXP_CLAMP))).astype(BF16)
        k_d = (k * jnp.exp(jnp.minimum(ref_d - cum, EXP_CLAMP))).astype(BF16)
        s_off = _dot_nt(q_off, k_off)
        s_diag = _dot_nt(q_d, k_d)
        cum_end = _row(cum, c_len - 1)
        q_dec = (q * jnp.exp(cum)).astype(BF16)
        k_dec = (k * jnp.exp(cum_end - cum)).astype(BF16)
        st = st_ref[hh]
        out_st = _dot_nt(q_dec, st.astype(BF16))
        st_new = st * jnp.exp(cum_end) + _dot_tn(v, k_dec)
        yield
        st_ref[hh] = st_new
        sc_ref[slot, hh] = (s_off + jnp.where(diag_mask, s_diag, 0.0)).astype(BF16)
        ost_ref[slot, hh] = out_st
        vb_ref[slot, hh] = v

    def phase_b(hh, r0, slot):
        rows = pl.ds(r0, c_len)
        cols = slice(hh * d, (hh + 1) * d)
        out = _dot(sc_ref[slot, hh], vb_ref[slot, hh]) + ost_ref[slot, hh]
        yield
        out = _rms(out, gn) * _silu(g_ref[rows, cols])
        o_ref[rows, cols] = out.astype(o_ref.dtype)

    return phase_a, phase_b


def _hgrn_scratch(hb):
    return [pltpu.VMEM((hb, HEAD_DIM, HEAD_DIM), F32),
            pltpu.VMEM((2, hb, CHUNK, CHUNK), BF16),
            pltpu.VMEM((2, hb, CHUNK, HEAD_DIM), F32),
            pltpu.VMEM((2, hb, CHUNK, HEAD_DIM), BF16)]


def _delta_phases(q_ref, k_ref, v_ref, z_ref, ab_ref, cw_ref, sc_ref, gn_ref, o_ref,
                  st_ref, hist_ref, win_ref, td_ref, nm_ref, rhs_ref, in_ref, qd_ref, kd_ref, dec_ref,
                  first_block, n_chunks):
    c_len, d = CHUNK, HEAD_DIM
    head0 = 0

    @pl.when(first_block)
    def _():
        st_ref[...] = jnp.zeros_like(st_ref)
        hist_ref[...] = jnp.zeros_like(hist_ref)

    rt = lax.broadcasted_iota(jnp.int32, (c_len, c_len), 0)
    rs = lax.broadcasted_iota(jnp.int32, (c_len, c_len), 1)
    causal = rs <= rt
    strict = rs < rt
    tril = causal.astype(BF16)
    ones_cc = jnp.ones((c_len, c_len), BF16)
    eye = (rs == rt).astype(F32)
    bt, bs = rt // 16, rs // 16
    m_diag = jnp.logical_and(strict, bt == bs)
    m_l1 = jnp.logical_and(bt // 2 == bs // 2, bt == bs + 1)
    m_l2 = jnp.logical_and(bt >= 2, bs < 2)
    lane = lax.broadcasted_iota(jnp.int32, (c_len, d), 1)
    gn = gn_ref[...]

    def conv(ref, which, hh, c, r0):
        cols = slice(hh * d, (hh + 1) * d)
        cw = cw_ref[hh, which]
        prev0 = pl.multiple_of(jnp.maximum(r0 - 8, 0), 8)
        prev = jnp.where(c > 0, ref[pl.ds(prev0, 8), cols], hist_ref[which, :, cols])
        cur = ref[pl.ds(r0, c_len), cols]
        win_ref[which, hh, 0:8, :] = prev
        win_ref[which, hh, 8:, :] = cur
        acc = cur * cw[CONV_WIDTH - 1:CONV_WIDTH, :]
        for j in range(1, CONV_WIDTH):
            shifted = win_ref[which, hh, 8 - j:8 - j + c_len, :]
            acc = acc + shifted * cw[CONV_WIDTH - 1 - j:CONV_WIDTH - j, :]
        return _silu(acc)

    def mm(a, b):
        return _dot(a.astype(BF16), b.astype(BF16))

    def chunk_gates(r0):
        ab = ab_ref[pl.ds(r0, c_len), :]
        x = ab + sc_ref[1:2, :]
        softplus = jnp.maximum(x, 0.0) + jnp.log(1.0 + jnp.exp(-jnp.abs(x)))
        return -jnp.exp(sc_ref[0:1, :]) * softplus, _sigmoid(ab)

    def phase_a(hh, c, r0, slot, gates):
        cols = slice(hh * d, (hh + 1) * d)
        head = head0 + hh
        log_a_all, beta_all = gates
        q = conv(q_ref, 0, hh, c, r0)
        k = conv(k_ref, 1, hh, c, r0)
        v = conv(v_ref, 2, hh, c, r0)
        q = q * lax.rsqrt(jnp.sum(q * q, axis=-1, keepdims=True) + NORM_EPS) * (d ** -0.5)
        k = k * lax.rsqrt(jnp.sum(k * k, axis=-1, keepdims=True) + NORM_EPS)
        yield

        la_col = jnp.sum(jnp.where(lane == head, log_a_all, 0.0), axis=-1, keepdims=True)
        beta = jnp.sum(jnp.where(lane == head + N_HEADS, beta_all, 0.0), axis=-1, keepdims=True)
        log_a = jnp.broadcast_to(la_col, (c_len, d))
        cum = _dot_split(tril, log_a)
        cum_row = _dot_split(ones_cc, jnp.where(rs >= rt, log_a[:, :c_len], 0.0))
        k_bf = k.astype(BF16)
        k_beta = k * beta
        kk = _dot_nt(k_beta.astype(BF16), k_bf)
        qk = _dot_nt(q.astype(BF16), k_bf)
        yield
        decay = jnp.exp(jnp.where(causal, cum[:, :c_len] - cum_row, 0.0))
        n_mat = jnp.where(strict, kk * decay, 0.0)

        n_d = jnp.where(m_diag, n_mat, 0.0)
        p2 = mm(n_d, n_d)
        e_cum = jnp.exp(cum)
        cum_end = _row(cum, c_len - 1)
        rhs_ref[slot, hh] = jnp.concatenate([v * beta, k_beta * e_cum], axis=1).astype(BF16)
        qd_ref[slot, hh] = (q * e_cum).astype(BF16)
        kd_ref[slot, hh] = (k * jnp.exp(cum_end - cum)).astype(BF16)
        dec_ref[slot, hh] = jnp.exp(cum_end)
        in_ref[slot, hh] = jnp.where(causal, qk * decay, 0.0).astype(BF16)
        nm_ref[slot, hh] = n_mat
        yield
        p4 = mm(p2, p2)
        a12 = mm(eye - n_d, eye + p2)
        yield
        p8 = mm(p4, p4)
        yield
        a48 = mm(eye + p4, eye + p8)
        yield
        td_ref[slot, hh] = mm(a12, a48)

    def phase_b(hh, r0, slot):
        rows = pl.ds(r0, c_len)
        cols = slice(hh * d, (hh + 1) * d)
        t_d = td_ref[slot, hh]
        n_mat = nm_ref[slot, hh]
        x1 = mm(t_d, jnp.where(m_l1, n_mat, 0.0))
        yield
        t_32 = t_d - mm(x1, t_d)
        yield
        x2 = mm(t_32, jnp.where(m_l2, n_mat, 0.0))
        yield
        t_inv = t_32 - mm(x2, t_32)
        yield
        sol = _dot(t_inv.astype(BF16), rhs_ref[slot, hh])
        yield
        u, w = sol[:, :d], sol[:, d:]
        st = st_ref[hh]
        st_bf = st.astype(BF16)
        v_new = u - _dot(w.astype(BF16), st_bf)
        out_st = _dot(qd_ref[slot, hh], st_bf)
        yield
        v_new_bf = v_new.astype(BF16)
        out = out_st + _dot(in_ref[slot, hh], v_new_bf)
        st_ref[hh] = st * dec_ref[slot, hh] + _dot_tn(kd_ref[slot, hh], v_new_bf)
        yield
        out = _rms(out, gn) * _silu(z_ref[rows, cols])
        o_ref[rows, cols] = out.astype(o_ref.dtype)

    def save_history():
        last8 = pl.ds(n_chunks * c_len - 8, 8)
        hist_ref[0] = q_ref[last8, :]
        hist_ref[1] = k_ref[last8, :]
        hist_ref[2] = v_ref[last8, :]

    return chunk_gates, phase_a, phase_b, save_history


def _delta_scratch(hb):
    return [pltpu.VMEM((hb, HEAD_DIM, HEAD_DIM), F32),
            pltpu.VMEM((3, 8, hb * HEAD_DIM), F32),
            pltpu.VMEM((3, hb, 8 + CHUNK, HEAD_DIM), F32),
            pltpu.VMEM((2, hb, CHUNK, CHUNK), F32),
            pltpu.VMEM((2, hb, CHUNK, CHUNK), F32),
            pltpu.VMEM((2, hb, CHUNK, 2 * HEAD_DIM), BF16),
            pltpu.VMEM((2, hb, CHUNK, CHUNK), BF16),
            pltpu.VMEM((2, hb, CHUNK, HEAD_DIM), BF16),
            pltpu.VMEM((2, hb, CHUNK, HEAD_DIM), BF16),
            pltpu.VMEM((2, hb, 1, HEAD_DIM), F32)]


N_HGRN_IN, N_DELTA_IN = 6, 8
N_HGRN_SCRATCH, N_DELTA_SCRATCH = 4, 10


def _mixers_kernel(*refs, n_chunks):
    c_len, hb = CHUNK, N_HEADS
    hgrn_in, refs = refs[:N_HGRN_IN], refs[N_HGRN_IN:]
    delta_in, refs = refs[:N_DELTA_IN], refs[N_DELTA_IN:]
    (oa_ref, ob_ref), refs = refs[:2], refs[2:]
    hgrn_scr, delta_scr = refs[:N_HGRN_SCRATCH], refs[N_HGRN_SCRATCH:]
    first_block = pl.program_id(1) == 0
    h_a, h_b = _hgrn_phases(*hgrn_in, oa_ref, *hgrn_scr, first_block)
    gates_of, d_a, d_b, save_history = _delta_phases(*delta_in, ob_ref, *delta_scr, first_block, n_chunks)
    heads = range(hb)

    def first_chunk(c, carry):
        r0 = pl.multiple_of(c * c_len, c_len)
        gates = gates_of(r0)
        _run_lockstep([d_a(hh, c, r0, c % 2, gates) for hh in heads]
                      + [h_a(hh, c, r0, c % 2) for hh in heads])
        return carry

    def chunk(c, carry):
        r0 = pl.multiple_of(c * c_len, c_len)
        prev = pl.multiple_of(r0 - c_len, c_len)
        slot = c % 2
        gates = gates_of(r0)
        _run_lockstep([d_b(hh, prev, 1 - slot) for hh in heads]
                      + [h_b(hh, prev, 1 - slot) for hh in heads]
                      + [d_a(hh, c, r0, slot, gates) for hh in heads]
                      + [h_a(hh, c, r0, slot) for hh in heads])
        return carry

    lax.fori_loop(0, 1, first_chunk, 0)
    lax.fori_loop(1, n_chunks, chunk, 0)
    last, last_slot = (n_chunks - 1) * c_len, (n_chunks - 1) % 2
    _run_lockstep([d_b(hh, last, last_slot) for hh in heads] + [h_b(hh, last, last_slot) for hh in heads])
    save_history()


def _mixers(proj, small, lb, g_norm_a, conv_w, scal, g_norm_b, *, bsz, seq, sb=512):
    sb = min(sb, seq)
    hb = N_HEADS
    wb = hb * HEAD_DIM
    col = lambda k: pl.BlockSpec((None, sb, wb), lambda b, s: (b, s, k))
    const = lambda shape: pl.BlockSpec(shape, lambda b, s: (0,) * len(shape))
    out = jax.ShapeDtypeStruct((bsz, seq, wb), BF16)
    return pl.pallas_call(
        functools.partial(_mixers_kernel, n_chunks=sb // CHUNK),
        out_shape=(out, out),
        grid=(bsz, seq // sb),
        in_specs=[col(0), col(1), col(2), col(3), const((hb, 1, HEAD_DIM)), const((1, HEAD_DIM)),
                  col(4), col(5), col(6), col(7),
                  pl.BlockSpec((None, sb, HEAD_DIM), lambda b, s: (b, s, 0)),
                  const((hb, 3, CONV_WIDTH, HEAD_DIM)), const((2, HEAD_DIM)), const((1, HEAD_DIM))],
        out_specs=(col(0), col(0)),
        scratch_shapes=_hgrn_scratch(hb) + _delta_scratch(hb),
        compiler_params=_params("parallel", "arbitrary"),
        name="mixers",
    )(proj, proj, proj, proj, lb, g_norm_a, proj, proj, proj, proj, small, conv_w, scal, g_norm_b)


def _cmul(ar, ai, br, bi):
    return ar * br - ai * bi, ar * bi + ai * br


def _s5_expand(a, g8):
    gs, lanes = S5_GROUP, S5_PACK * S5_GROUP
    sel = (lax.broadcasted_iota(jnp.int32, (a.shape[0], lanes), 1) // gs) == g8
    pieces = []
    for k in range(S5_L):
        src = a[:, (k // S5_PACK) * lanes:(k // S5_PACK + 1) * lanes]
        shift = ((g8 - k % S5_PACK + S5_PACK) * gs) % lanes
        pieces.append(jnp.where(sel, pltpu.roll(src, shift, axis=1), 0.0))
    return jnp.concatenate(pieces, axis=1)


def _s5_prep_kernel(are_ref, aim_ref, ldt_ref, bre_ref, bim_ref, cre_ref, cim_ref,
                    are2_ref, aim2_ref, ldt2_ref, w8_ref, m1_ref, m2_ref, lam_ref):
    def one_group(g8, carry):
        _s5_prep_group(g8, are_ref[g8], aim_ref[g8], ldt_ref[g8], bre_ref[g8], bim_ref[g8],
                       cre_ref[g8], cim_ref[g8], are2_ref[g8], aim2_ref[g8], ldt2_ref[g8],
                       w8_ref, m1_ref, m2_ref, lam_ref)
        return carry

    lax.fori_loop(0, S5_PACK, one_group, 0)


def _s5_prep_group(g8, are, aim, ldt, bre, bim, cre, cim, are2, aim2, ldt2,
                   w8_ref, m1_ref, m2_ref, lam_ref):
    p, w, l, gs = S5_STATE, S5_W, S5_L, S5_GROUP
    lanes = S5_PACK * gs
    hp = lax.Precision.HIGHEST
    dt = jnp.exp(ldt)
    a_re = jnp.broadcast_to(are, (p, w))
    a_im = jnp.broadcast_to(aim, (p, w))

    e_lane = jnp.minimum(lax.broadcasted_iota(jnp.int32, (p, lanes), 1), l).astype(F32)
    lr, li = e_lane * (are * dt), e_lane * (aim * dt)
    mag = jnp.exp(lr)
    powers = jnp.concatenate([mag * jnp.cos(li), mag * jnp.sin(li)], axis=0)
    sel_e = lax.broadcasted_iota(jnp.int32, (lanes, w), 0)
    sel_k = lax.broadcasted_iota(jnp.int32, (lanes, w), 1) // gs
    exps = (lambda k: k, lambda k: l - 1 - k, lambda k: k + 1)
    sel = jnp.concatenate([(sel_e == f(sel_k)).astype(BF16) for f in exps], axis=1)
    tiled, rest = 0.0, powers
    for _ in range(3):
        part = rest.astype(BF16)
        tiled = tiled + _dot(part, sel)
        rest = rest - part.astype(F32)

    def lam_pow(i):
        return tiled[:p, i * w:(i + 1) * w], tiled[p:, i * w:(i + 1) * w]

    lb_re = jnp.broadcast_to(powers[:p, 1:2], (p, w))
    lb_im = jnp.broadcast_to(powers[p:, 1:2], (p, w))
    den = a_re * a_re + a_im * a_im
    xr, xi = lb_re - 1.0, lb_im
    coef_re, coef_im = (xr * a_re + xi * a_im) / den, (xi * a_re - xr * a_im) / den
    bb_re, bb_im = _cmul(coef_re, coef_im, bre, bim)

    e_re, e_im = _cmul(*lam_pow(0), cre, cim)
    lhs = jnp.concatenate([bb_re[:, :gs], -bb_im[:, :gs]], axis=0)
    rhs = jnp.concatenate([e_re, e_im], axis=0)
    r0 = lax.dot_general(lhs, rhs, (((0,), (0,)), ((), ())), precision=hp,
                         preferred_element_type=F32)
    r0x = _s5_expand(r0, g8).astype(w8_ref.dtype)
    for s in range(l):
        rows = pl.ds(pl.multiple_of(s * lanes + g8 * gs, gs), gs)
        if s:
            w8_ref[rows, :s * lanes] = jnp.zeros((gs, s * lanes), w8_ref.dtype)
        w8_ref[rows, s * lanes:] = r0x[:, :(l - s) * lanes]

    st_rows = pl.ds(pl.multiple_of(g8 * 2 * p, 2 * p), 2 * p)
    d_re, d_im = _cmul(*lam_pow(1), bb_re, bb_im)
    m1_ref[st_rows, :] = _s5_expand(jnp.concatenate([d_re, d_im], axis=0), g8).astype(m1_ref.dtype)
    f_re, f_im = _cmul(*lam_pow(2), cre, cim)
    m2_ref[st_rows, :] = _s5_expand(jnp.concatenate([f_re, -f_im], axis=0), g8).astype(m2_ref.dtype)

    dt2 = jnp.exp(ldt2)
    mag_l = jnp.exp(float(l) * are2 * dt2)
    ang = float(l) * aim2 * dt2
    ll_re, ll_im = mag_l * jnp.cos(ang), mag_l * jnp.sin(ang)
    first = lax.broadcasted_iota(jnp.int32, ll_im.shape, 1) < p
    lam_ref[g8] = jnp.concatenate([ll_re, jnp.where(first, -ll_im, ll_im)]
                                  + [jnp.zeros_like(ll_re)] * 6, axis=0)


def _s5_prep(a_re, a_im, log_dt, b_re, b_im, c_re, c_im):
    g, p = a_re.shape
    w, l, pk = S5_W, S5_L, S5_PACK
    wx = l * pk * S5_GROUP
    col = lambda x: x.reshape(g, p, 1)
    tile = lambda x: jnp.tile(x, (1, 1, l))
    dup = lambda x: jnp.concatenate([x, x], axis=-1).reshape(g, 1, 2 * p)
    ldt2 = jnp.broadcast_to(log_dt.reshape(g, 1, 1), (g, 1, 2 * p))
    args = (col(a_re), col(a_im), log_dt.reshape(g, 1, 1), tile(b_re), tile(b_im),
            tile(jnp.swapaxes(c_re, 1, 2)), tile(jnp.swapaxes(c_im, 1, 2)),
            dup(a_re), dup(a_im), ldt2)

    def gspec(shape):
        return pl.BlockSpec((pk,) + shape, lambda i: (i,) + (0,) * len(shape))

    def ospec(shape):
        return pl.BlockSpec((None,) + shape, lambda i: (i,) + (0,) * len(shape))

    return pl.pallas_call(
        _s5_prep_kernel,
        out_shape=(jax.ShapeDtypeStruct((g // pk, wx, wx), BF16),
                   jax.ShapeDtypeStruct((g // pk, pk * 2 * p, wx), BF16),
                   jax.ShapeDtypeStruct((g // pk, pk * 2 * p, wx), BF16),
                   jax.ShapeDtypeStruct((g, 8, 2 * p), F32)),
        grid=(g // pk,),
        in_specs=[gspec((p, 1)), gspec((p, 1)), gspec((1, 1)), gspec((p, w)), gspec((p, w)),
                  gspec((p, w)), gspec((p, w)), gspec((1, 2 * p)), gspec((1, 2 * p)),
                  gspec((1, 2 * p))],
        out_specs=(ospec((wx, wx)), ospec((pk * 2 * p, wx)), ospec((pk * 2 * p, wx)),
                   gspec((8, 2 * p))),
        compiler_params=_params("parallel"),
        name="s5_prep",
    )(*args)


def _s5_gather(u_ref, x8_ref, nbk):
    lanes = u_ref.shape[1]
    for t in range(S5_L):
        x8_ref[:, t * lanes:(t + 1) * lanes] = u_ref[pl.ds(t, nbk, stride=S5_L), :].astype(BF16)


def _s5_inc_kernel(u_ref, m1_ref, inc_ref, x8_ref):
    _s5_gather(u_ref, x8_ref, inc_ref.shape[0])
    inc_ref[...] = _dot_nt(x8_ref[...], m1_ref[...])


def _s5_inc(u, m1, *, rb=2):
    t, d = u.shape
    ngb, sp, wx = m1.shape
    lanes = d // ngb
    rows = t // rb
    nbk = rows // S5_L
    return pl.pallas_call(
        _s5_inc_kernel,
        out_shape=jax.ShapeDtypeStruct((t // S5_L, ngb * sp), F32),
        grid=(ngb, rb),
        in_specs=[pl.BlockSpec((rows, lanes), lambda i, r: (r, i)),
                  pl.BlockSpec((None, sp, wx), lambda i, r: (i, 0, 0))],
        out_specs=pl.BlockSpec((nbk, sp), lambda i, r: (r, i)),
        scratch_shapes=[pltpu.VMEM((nbk, wx), BF16)],
        compiler_params=_params("parallel", "arbitrary"),
        name="s5_inc",
    )(u, m1)


def _s5_scan_kernel(inc_ref, lam_ref, x_ref, *, bsz, n_steps):
    lam = lam_ref[...]
    a, bc = lam[0:1, :], lam[1:2, :]
    width = inc_ref.shape[1]
    sp = 2 * S5_STATE
    per = SUBLANES // bsz
    first = (lax.broadcasted_iota(jnp.int32, (SUBLANES, width), 1) % sp) < S5_STATE
    row = lax.broadcasted_iota(jnp.int32, (SUBLANES, width), 0)

    def swap(v):
        return jnp.where(first, pltpu.roll(v, width - S5_STATE, axis=1),
                         pltpu.roll(v, S5_STATE, axis=1))

    def tile_step(m, carry):
        x, xs = carry
        rows = pl.ds(pl.multiple_of(m * SUBLANES, SUBLANES), SUBLANES)
        inc_tile = inc_ref[rows, :]
        out = x
        for j in range(per):
            inc = inc_tile if j == 0 else pltpu.roll(inc_tile, SUBLANES - j * bsz, axis=0)
            x, xs = a * x + bc * xs + inc, a * xs - bc * x + swap(inc)
            if j + 1 < per:
                out = jnp.where(row < (j + 1) * bsz, out, pltpu.roll(x, (j + 1) * bsz, axis=0))
        x_ref[rows, :] = out
        return x, xs

    zero = jnp.zeros((SUBLANES, width), F32)
    lax.fori_loop(0, n_steps // per, tile_step, (zero, zero), unroll=4)


def _s5_scan(inc, lam_rows, *, bsz, wb=2048):
    n, width = inc.shape
    blk = pl.BlockSpec((n, wb), lambda i: (0, i))
    return pl.pallas_call(
        functools.partial(_s5_scan_kernel, bsz=bsz, n_steps=n // bsz),
        out_shape=jax.ShapeDtypeStruct((n, width), F32),
        grid=(width // wb,),
        in_specs=[blk, pl.BlockSpec((8, wb), lambda i: (0, i))],
        out_specs=blk,
        compiler_params=_params("parallel"),
        name="s5_scan",
    )(inc, lam_rows)


def _s5_out_kernel(u_ref, x_ref, w8_ref, m2_ref, d_ref, o_ref, x8_ref):
    nbk = x_ref.shape[0]
    lanes = u_ref.shape[1]
    gelu_c = math.sqrt(2.0 / math.pi)
    d_skip = d_ref[...]
    _s5_gather(u_ref, x8_ref, nbk)
    y8 = _dot(x8_ref[...], w8_ref[...]) + _dot(x_ref[...].astype(BF16), m2_ref[...])
    for t in range(S5_L):
        rows = pl.ds(t, nbk, stride=S5_L)
        y = y8[:, t * lanes:(t + 1) * lanes] + d_skip * u_ref[rows, :]
        o_ref[rows, :] = 0.5 * y * (1.0 + jnp.tanh(gelu_c * (y + 0.044715 * (y * y * y))))


def _s5_out(u, x_all, w8, m2, d_skip, *, rb=2):
    t, d = u.shape
    ngb, sp, wx = m2.shape
    lanes = d // ngb
    rows = t // rb
    nbk = rows // S5_L
    return pl.pallas_call(
        _s5_out_kernel,
        out_shape=jax.ShapeDtypeStruct((t, d), F32),
        grid=(ngb, rb),
        in_specs=[pl.BlockSpec((rows, lanes), lambda i, r: (r, i)),
                  pl.BlockSpec((nbk, sp), lambda i, r: (r, i)),
                  pl.BlockSpec((None, wx, wx), lambda i, r: (i, 0, 0)),
                  pl.BlockSpec((None, sp, wx), lambda i, r: (i, 0, 0)),
                  pl.BlockSpec((1, lanes), lambda i, r: (0, i))],
        out_specs=pl.BlockSpec((rows, lanes), lambda i, r: (r, i)),
        scratch_shapes=[pltpu.VMEM((nbk, wx), BF16)],
        compiler_params=_params("parallel", "arbitrary"),
        name="s5_out",
    )(u, x_all, w8, m2, d_skip)


def _s5_act(u, a_re, a_im, b_re, b_im, c_re, c_im, d_skip, log_dt, *, bsz):
    d = u.shape[1]
    g = d // S5_GROUP
    w8, m1, m2, lam = _s5_prep(a_re, a_im, log_dt, b_re, b_im, c_re, c_im)
    lam_rows = jnp.swapaxes(lam, 0, 1).reshape(8, g * 2 * S5_STATE)
    inc = _s5_inc(u, m1)
    x_all = _s5_scan(inc, lam_rows, bsz=bsz)
    return _s5_out(u, x_all, w8, m2, d_skip.reshape(1, d).astype(F32))


def kernel(x, p, norm_mix, norm_mlp, norm_ple, w_in_e, w_out_e, hgrn_lb, g_norm_a, conv_w, a_log, dt_bias, g_norm_b, s5_a_re, s5_a_im, s5_b_re, s5_b_im, s5_c_re, s5_c_im, s5_d, s5_log_dt, w_glu, b_glu, w_out_o, w_up, w_down, w_ple_gate, w_ple_proj, final_norm):
    bsz, seq, d = x.shape
    t = bsz * seq
    depth = p.shape[0]
    heads, hd = N_HEADS, HEAD_DIM
    width = heads * hd
    main_cols = 8 * width
    lower_bounds = jnp.cumsum(jax.nn.softmax(hgrn_lb.astype(F32), axis=0), axis=0)
    row = lambda v: v.reshape(1, -1).astype(F32)

    h = x.reshape(t, d)
    p_bf = p.reshape(depth, t, -1).astype(BF16)
    w_in_bf, w_out_e_bf, w_out_o_bf, w_glu_bf = (w.astype(BF16) for w in (w_in_e, w_out_e, w_out_o, w_glu))
    w_up_bf, w_down_bf, w_gate_bf, w_proj_bf = (w.astype(BF16) for w in (w_up, w_down, w_ple_gate, w_ple_proj))
    out = None
    for i in range(depth):
        j = i // 2
        if i % 2 == 0:
            w_small = jnp.pad(w_in_e[j, :, main_cols:], ((0, 0), (0, hd - 2 * heads))).astype(BF16)
            proj, small = _in_proj(h, row(norm_mix[i]), w_in_bf, j, main_cols, w_small)
            proj = proj.reshape(bsz, seq, main_cols)
            small = small.reshape(bsz, seq, hd)
            cw = conv_w[j].reshape(CONV_WIDTH, 3, heads, hd).transpose(2, 1, 0, 3)
            scal = jnp.pad(jnp.stack([a_log[j], dt_bias[j]]).astype(F32), ((0, 0), (0, hd - heads)))
            o_a, o_b = _mixers(proj, small, lower_bounds[i].reshape(heads, 1, hd), row(g_norm_a[j]),
                               cw, scal, row(g_norm_b[j]), bsz=bsz, seq=seq)
            mix_in = ([o_a.reshape(t, width), o_b.reshape(t, width)], w_out_e_bf, j)
            mix_seq = None
        else:
            act = _s5_act(normed.reshape(t, d), s5_a_re[j], s5_a_im[j], s5_b_re[j], s5_b_im[j],
                          s5_c_re[j], s5_c_im[j], s5_d[j], s5_log_dt[j], bsz=bsz)
            glu = _glu(act, w_glu_bf, j, row(b_glu[j]))
            mix_in = ([glu.reshape(seq // S5_L, bsz, S5_L, d)], w_out_o_bf, j)
            mix_seq = seq
        h = _mix_mlp(*mix_in, h, row(norm_mlp[i]), w_up_bf, w_down_bf, i, chunk_major_seq=mix_seq)
        last = i == depth - 1
        feeds_s5 = not last and (i + 1) % 2 == 1
        h, normed = _ple(h, row(norm_ple[i]), w_gate_bf, p_bf, w_proj_bf,
                         row(final_norm if last else norm_mix[i + 1]), i,
                         chunk_major=(bsz, seq) if feeds_s5 else None)
        out = normed
    return out.reshape(bsz, seq, d)
```

```python
import functools
import math

import jax
import jax.numpy as jnp
from jax import lax
from jax.experimental import pallas as pl
from jax.experimental.pallas import tpu as pltpu

F32 = jnp.float32
BF16 = jnp.bfloat16

SUBLANES = 8
NORM_EPS = 1e-6
CHUNK = 64
HEAD_DIM = 128
N_HEADS = 8
CONV_WIDTH = 4
S5_GROUP = 16
S5_STATE = 64
S5_L = 16
S5_W = S5_L * S5_GROUP
S5_PACK = 128 // S5_GROUP
EXP_CLAMP = 80.0

VMEM_LIMIT = 56 * 1024 * 1024


def _sigmoid(x):
    return 0.5 * jnp.tanh(0.5 * x) + 0.5


def _silu(x):
    return x * _sigmoid(x)


def _rms(x, g):
    return x * lax.rsqrt(jnp.mean(x * x, axis=-1, keepdims=True) + NORM_EPS) * g


def _dot(a, b):
    return jnp.dot(a, b, preferred_element_type=F32)


def _dot_nt(a, b):
    return lax.dot_general(a, b, (((1,), (1,)), ((), ())), preferred_element_type=F32)


def _dot_tn(a, b):
    return lax.dot_general(a, b, (((0,), (0,)), ((), ())), preferred_element_type=F32)


def _dot_split(a_bf, x):
    hi = x.astype(BF16)
    lo = (x - hi.astype(F32)).astype(BF16)
    return _dot(a_bf, hi) + _dot(a_bf, lo)


def _params(*sem):
    return pltpu.CompilerParams(dimension_semantics=sem, vmem_limit_bytes=VMEM_LIMIT)


def _in_proj_kernel(x_ref, g_ref, w_ref, ws_ref, o_ref, os_ref, hn_ref):
    @pl.when(pl.program_id(1) == 0)
    def _():
        hn = _rms(x_ref[...], g_ref[...]).astype(BF16)
        hn_ref[...] = hn
        os_ref[...] = _dot(hn, ws_ref[...])

    o_ref[...] = _dot(hn_ref[...], w_ref[...])


def _in_proj(x, g, w, layer, n, w_small, *, tm=1024, tn=1024):
    t, d = x.shape
    tm = min(tm, t)
    ns = w_small.shape[1]
    return pl.pallas_call(
        _in_proj_kernel,
        out_shape=(jax.ShapeDtypeStruct((t, n), F32), jax.ShapeDtypeStruct((t, ns), F32)),
        grid=(t // tm, n // tn),
        in_specs=[pl.BlockSpec((tm, d), lambda i, j: (i, 0)),
                  pl.BlockSpec((1, d), lambda i, j: (0, 0)),
                  pl.BlockSpec((None, d, tn), lambda i, j: (layer, 0, j)),
                  pl.BlockSpec((d, ns), lambda i, j: (0, 0))],
        out_specs=(pl.BlockSpec((tm, tn), lambda i, j: (i, j)),
                   pl.BlockSpec((tm, ns), lambda i, j: (i, 0))),
        scratch_shapes=[pltpu.VMEM((tm, d), BF16)],
        compiler_params=_params("parallel", "arbitrary"),
        name="in_proj",
    )(x, g, w, w_small)


def _chunk_major_spec(tm, width, seq, **kw):
    per_seq = seq // tm
    return pl.BlockSpec((tm // S5_L, None, S5_L, width),
                        lambda i, *_: (i % per_seq, i // per_seq, 0, 0), **kw)


def _resident(shape, layer=None, block=0):
    if layer is None:
        return pl.BlockSpec(shape, lambda *_: (0,) * len(shape), pipeline_mode=pl.Buffered(1))
    return pl.BlockSpec((None,) + tuple(shape), lambda *_: (layer, block, 0),
                        pipeline_mode=pl.Buffered(1))


def _mix_mlp_kernel(*refs, n_in):
    x_refs, w_refs = refs[:n_in], refs[n_in:2 * n_in]
    r_ref, g_ref, wu_ref, wd_ref, o_ref, hn_ref = refs[2 * n_in:]

    @pl.when(pl.program_id(1) == 0)
    def _():
        h = r_ref[...]
        for x_ref, w_ref in zip(x_refs, w_refs):
            h = h + _dot(x_ref[...].reshape(-1, x_ref.shape[-1]), w_ref[...])
        hn_ref[...] = _rms(h, g_ref[...]).astype(BF16)
        o_ref[...] = h

    a = jnp.maximum(_dot(hn_ref[...], wu_ref[...]), 0.0)
    o_ref[...] += _dot((a * a).astype(BF16), wd_ref[...])


def _mix_mlp(xs, w_mix, mix_layer, resid, g, w_up, w_down, layer, *, chunk_major_seq=None,
             tm=512, tf=1024):
    t, d = resid.shape
    f = w_up.shape[2]
    n_in = len(xs)
    kx = w_mix.shape[1] // n_in
    if chunk_major_seq is None:
        x_specs = [pl.BlockSpec((tm, x.shape[1]), lambda i, j: (i, 0)) for x in xs]
    else:
        x_specs = [_chunk_major_spec(tm, x.shape[-1], chunk_major_seq) for x in xs]
    return pl.pallas_call(
        functools.partial(_mix_mlp_kernel, n_in=n_in),
        out_shape=jax.ShapeDtypeStruct((t, d), F32),
        grid=(t // tm, f // tf),
        in_specs=(x_specs + [_resident((kx, d), mix_layer, k) for k in range(n_in)]
                  + [pl.BlockSpec((tm, d), lambda i, j: (i, 0)),
                     pl.BlockSpec((1, d), lambda i, j: (0, 0)),
                     pl.BlockSpec((None, d, tf), lambda i, j: (layer, 0, j)),
                     pl.BlockSpec((None, tf, d), lambda i, j: (layer, j, 0))]),
        out_specs=pl.BlockSpec((tm, d), lambda i, j: (i, 0)),
        scratch_shapes=[pltpu.VMEM((tm, d), BF16)],
        compiler_params=_params("parallel", "arbitrary"),
        name="mix_mlp",
    )(*xs, *([w_mix] * n_in), resid, g, w_up, w_down)


def _ple_kernel(h_ref, g_ref, wg_ref, p_ref, wp_ref, g2_ref, o_ref, on_ref):
    h = h_ref[...]
    gate = _sigmoid(_dot(_rms(h, g_ref[...]).astype(BF16), wg_ref[...]))
    h_new = h + gate * _dot(p_ref[...], wp_ref[...])
    o_ref[...] = h_new
    on_ref[...] = _rms(h_new, g2_ref[...]).reshape(on_ref.shape)


def _ple(h, g, w_gate, p, w_proj, g_next, layer, *, chunk_major=None, tm=512):
    t, d = h.shape
    pd = p.shape[2]
    if chunk_major is None:
        normed_shape, normed_spec = (t, d), pl.BlockSpec((tm, d), lambda i: (i, 0))
    else:
        bsz, seq = chunk_major
        normed_shape, normed_spec = (seq // S5_L, bsz, S5_L, d), _chunk_major_spec(tm, d, seq)
    return pl.pallas_call(
        _ple_kernel,
        out_shape=(jax.ShapeDtypeStruct((t, d), F32), jax.ShapeDtypeStruct(normed_shape, F32)),
        grid=(t // tm,),
        in_specs=[pl.BlockSpec((tm, d), lambda i: (i, 0)),
                  _resident((1, d)),
                  _resident((d, d), layer),
                  pl.BlockSpec((None, tm, pd), lambda i: (layer, i, 0)),
                  _resident((pd, d), layer),
                  _resident((1, d))],
        out_specs=(pl.BlockSpec((tm, d), lambda i: (i, 0)), normed_spec),
        compiler_params=_params("parallel"),
        name="ple",
    )(h, g, w_gate, p, w_proj, g_next)


def _glu_kernel(a_ref, w_ref, b_ref, o_ref):
    a = a_ref[...]
    z = _dot(a.astype(BF16), w_ref[...]) + b_ref[...]
    o_ref[...] = (a * _sigmoid(z)).astype(BF16)


def _glu(act, w, layer, b, *, tm=512):
    t, d = act.shape
    return pl.pallas_call(
        _glu_kernel,
        out_shape=jax.ShapeDtypeStruct((t, d), BF16),
        grid=(t // tm,),
        in_specs=[pl.BlockSpec((tm, d), lambda i: (i, 0)), _resident((d, d), layer),
                  _resident((1, d))],
        out_specs=pl.BlockSpec((tm, d), lambda i: (i, 0)),
        compiler_params=_params("parallel"),
        name="glu",
    )(act, w, b)


def _row(x, t):
    return x[t:t + 1, :]


def _run_lockstep(gens):
    live = list(gens)
    while live:
        nxt = []
        for g in live:
            try:
                next(g)
                nxt.append(g)
            except StopIteration:
                pass
        live = nxt


def _hgrn_phases(q_ref, f_ref, i_ref, g_ref, lb_ref, gn_ref, o_ref, st_ref, sc_ref, ost_ref, vb_ref,
                 first_block):
    c_len, d = CHUNK, HEAD_DIM

    @pl.when(first_block)
    def _():
        st_ref[...] = jnp.zeros_like(st_ref)

    ri = lax.broadcasted_iota(jnp.int32, (c_len, d), 0)
    rt = lax.broadcasted_iota(jnp.int32, (c_len, c_len), 0)
    rs = lax.broadcasted_iota(jnp.int32, (c_len, c_len), 1)
    tril = (rs <= rt).astype(BF16)
    diag_mask = jnp.logical_and(rs <= rt, (rs // 16) == (rt // 16))
    gn = gn_ref[...]

    def phase_a(hh, c, r0, slot):
        del c
        rows = pl.ds(r0, c_len)
        cols = slice(hh * d, (hh + 1) * d)
        lb = lb_ref[hh]
        q = q_ref[rows, cols]
        forget = lb + (1.0 - lb) * _sigmoid(f_ref[rows, cols])
        k = 1.0 - forget
        v = i_ref[rows, cols].astype(BF16)
        cum = _dot_split(tril, jnp.log(forget))
        yield

        def side(valid, ref_row, sign, x):
            e = jnp.where(valid, sign * (cum - ref_row), 0.0)
            return jnp.where(valid, x * jnp.exp(e), 0.0)

        c31 = _row(cum, 31)
        ref_b = jnp.where(ri < 32, _row(cum, 15), _row(cum, 47))
        ref_d = jnp.where(ri < 16, _row(cum, 8),
                          jnp.where(ri < 32, _row(cum, 24),
                                    jnp.where(ri < 48, _row(cum, 40), _row(cum, 56))))
        hi16 = (ri % 32) >= 16
        q_b = side(hi16, ref_b, 1.0, q)
        k_b = side(jnp.logical_not(hi16), ref_b, -1.0, k)
        q_off = jnp.concatenate([side(ri >= 32, c31, 1.0, q),
                                 jnp.where(ri < 32, q_b, 0.0),
                                 jnp.where(ri >= 32, q_b, 0.0)], axis=1).astype(BF16)
        k_off = jnp.concatenate([side(ri < 32, c31, -1.0, k),
                                 jnp.where(ri < 32, k_b, 0.0),
                                 jnp.where(ri >= 32, k_b, 0.0)], axis=1).astype(BF16)
        q_d = (q * jnp.exp(jnp.minimum(cum - ref_d, EXP_CLAMP))).astype(BF16)
        k_d = (k * jnp.exp(jnp.minimum(ref_d - cum, EXP_CLAMP))).astype(BF16)
        s_off = _dot_nt(q_off, k_off)
        s_diag = _dot_nt(q_d, k_d)
        cum_end = _row(cum, c_len - 1)
        q_dec = (q * jnp.exp(cum)).astype(BF16)
        k_dec = (k * jnp.exp(cum_end - cum)).astype(BF16)
        st = st_ref[hh]
        out_st = _dot_nt(q_dec, st.astype(BF16))
        st_new = st * jnp.exp(cum_end) + _dot_tn(v, k_dec)
        yield
        st_ref[hh] = st_new
        sc_ref[slot, hh] = (s_off + jnp.where(diag_mask, s_diag, 0.0)).astype(BF16)
        ost_ref[slot, hh] = out_st
        vb_ref[slot, hh] = v

    def phase_b(hh, r0, slot):
        rows = pl.ds(r0, c_len)
        cols = slice(hh * d, (hh + 1) * d)
        out = _dot(sc_ref[slot, hh], vb_ref[slot, hh]) + ost_ref[slot, hh]
        yield
        out = _rms(out, gn) * _silu(g_ref[rows, cols])
        o_ref[rows, cols] = out.astype(o_ref.dtype)

    return phase_a, phase_b


def _hgrn_scratch(hb):
    return [pltpu.VMEM((hb, HEAD_DIM, HEAD_DIM), F32),
            pltpu.VMEM((2, hb, CHUNK, CHUNK), BF16),
            pltpu.VMEM((2, hb, CHUNK, HEAD_DIM), F32),
            pltpu.VMEM((2, hb, CHUNK, HEAD_DIM), BF16)]


def _delta_phases(q_ref, k_ref, v_ref, z_ref, ab_ref, cw_ref, sc_ref, gn_ref, o_ref,
                  st_ref, hist_ref, win_ref, td_ref, nm_ref, rhs_ref, in_ref, qd_ref, kd_ref, dec_ref,
                  first_block, n_chunks):
    c_len, d = CHUNK, HEAD_DIM
    head0 = 0

    @pl.when(first_block)
    def _():
        st_ref[...] = jnp.zeros_like(st_ref)
        hist_ref[...] = jnp.zeros_like(hist_ref)

    rt = lax.broadcasted_iota(jnp.int32, (c_len, c_len), 0)
    rs = lax.broadcasted_iota(jnp.int32, (c_len, c_len), 1)
    causal = rs <= rt
    strict = rs < rt
    tril = causal.astype(BF16)
    ones_cc = jnp.ones((c_len, c_len), BF16)
    eye = (rs == rt).astype(F32)
    bt, bs = rt // 16, rs // 16
    m_diag = jnp.logical_and(strict, bt == bs)
    m_l1 = jnp.logical_and(bt // 2 == bs // 2, bt == bs + 1)
    m_l2 = jnp.logical_and(bt >= 2, bs < 2)
    lane = lax.broadcasted_iota(jnp.int32, (c_len, d), 1)
    gn = gn_ref[...]

    def conv(ref, which, hh, c, r0):
        cols = slice(hh * d, (hh + 1) * d)
        cw = cw_ref[hh, which]
        prev0 = pl.multiple_of(jnp.maximum(r0 - 8, 0), 8)
        prev = jnp.where(c > 0, ref[pl.ds(prev0, 8), cols], hist_ref[which, :, cols])
        cur = ref[pl.ds(r0, c_len), cols]
        win_ref[which, hh, 0:8, :] = prev
        win_ref[which, hh, 8:, :] = cur
        acc = cur * cw[CONV_WIDTH - 1:CONV_WIDTH, :]
        for j in range(1, CONV_WIDTH):
            shifted = win_ref[which, hh, 8 - j:8 - j + c_len, :]
            acc = acc + shifted * cw[CONV_WIDTH - 1 - j:CONV_WIDTH - j, :]
        return _silu(acc)

    def mm(a, b):
        return _dot(a.astype(BF16), b.astype(BF16))

    def chunk_gates(r0):
        ab = ab_ref[pl.ds(r0, c_len), :]
        x = ab + sc_ref[1:2, :]
        softplus = jnp.maximum(x, 0.0) + jnp.log(1.0 + jnp.exp(-jnp.abs(x)))
        return -jnp.exp(sc_ref[0:1, :]) * softplus, _sigmoid(ab)

    def phase_a(hh, c, r0, slot, gates):
        cols = slice(hh * d, (hh + 1) * d)
        head = head0 + hh
        log_a_all, beta_all = gates
        q = conv(q_ref, 0, hh, c, r0)
        k = conv(k_ref, 1, hh, c, r0)
        v = conv(v_ref, 2, hh, c, r0)
        q = q * lax.rsqrt(jnp.sum(q * q, axis=-1, keepdims=True) + NORM_EPS) * (d ** -0.5)
        k = k * lax.rsqrt(jnp.sum(k * k, axis=-1, keepdims=True) + NORM_EPS)
        yield

        la_col = jnp.sum(jnp.where(lane == head, log_a_all, 0.0), axis=-1, keepdims=True)
        beta = jnp.sum(jnp.where(lane == head + N_HEADS, beta_all, 0.0), axis=-1, keepdims=True)
        log_a = jnp.broadcast_to(la_col, (c_len, d))
        cum = _dot_split(tril, log_a)
        cum_row = _dot_split(ones_cc, jnp.where(rs >= rt, log_a[:, :c_len], 0.0))
        k_bf = k.astype(BF16)
        k_beta = k * beta
        kk = _dot_nt(k_beta.astype(BF16), k_bf)
        qk = _dot_nt(q.astype(BF16), k_bf)
        yield
        decay = jnp.exp(jnp.where(causal, cum[:, :c_len] - cum_row, 0.0))
        n_mat = jnp.where(strict, kk * decay, 0.0)

        n_d = jnp.where(m_diag, n_mat, 0.0)
        p2 = mm(n_d, n_d)
        e_cum = jnp.exp(cum)
        cum_end = _row(cum, c_len - 1)
        rhs_ref[slot, hh] = jnp.concatenate([v * beta, k_beta * e_cum], axis=1).astype(BF16)
        qd_ref[slot, hh] = (q * e_cum).astype(BF16)
        kd_ref[slot, hh] = (k * jnp.exp(cum_end - cum)).astype(BF16)
        dec_ref[slot, hh] = jnp.exp(cum_end)
        in_ref[slot, hh] = jnp.where(causal, qk * decay, 0.0).astype(BF16)
        nm_ref[slot, hh] = n_mat
        yield
        p4 = mm(p2, p2)
        a12 = mm(eye - n_d, eye + p2)
        yield
        p8 = mm(p4, p4)
        yield
        a48 = mm(eye + p4, eye + p8)
        yield
        td_ref[slot, hh] = mm(a12, a48)

    def phase_b(hh, r0, slot):
        rows = pl.ds(r0, c_len)
        cols = slice(hh * d, (hh + 1) * d)
        t_d = td_ref[slot, hh]
        n_mat = nm_ref[slot, hh]
        x1 = mm(t_d, jnp.where(m_l1, n_mat, 0.0))
        yield
        t_32 = t_d - mm(x1, t_d)
        yield
        x2 = mm(t_32, jnp.where(m_l2, n_mat, 0.0))
        yield
        t_inv = t_32 - mm(x2, t_32)
        yield
        sol = _dot(t_inv.astype(BF16), rhs_ref[slot, hh])
        yield
        u, w = sol[:, :d], sol[:, d:]
        st = st_ref[hh]
        st_bf = st.astype(BF16)
        v_new = u - _dot(w.astype(BF16), st_bf)
        out_st = _dot(qd_ref[slot, hh], st_bf)
        yield
        v_new_bf = v_new.astype(BF16)
        out = out_st + _dot(in_ref[slot, hh], v_new_bf)
        st_ref[hh] = st * dec_ref[slot, hh] + _dot_tn(kd_ref[slot, hh], v_new_bf)
        yield
        out = _rms(out, gn) * _silu(z_ref[rows, cols])
        o_ref[rows, cols] = out.astype(o_ref.dtype)

    def save_history():
        last8 = pl.ds(n_chunks * c_len - 8, 8)
        hist_ref[0] = q_ref[last8, :]
        hist_ref[1] = k_ref[last8, :]
        hist_ref[2] = v_ref[last8, :]

    return chunk_gates, phase_a, phase_b, save_history


def _delta_scratch(hb):
    return [pltpu.VMEM((hb, HEAD_DIM, HEAD_DIM), F32),
            pltpu.VMEM((3, 8, hb * HEAD_DIM), F32),
            pltpu.VMEM((3, hb, 8 + CHUNK, HEAD_DIM), F32),
            pltpu.VMEM((2, hb, CHUNK, CHUNK), F32),
            pltpu.VMEM((2, hb, CHUNK, CHUNK), F32),
            pltpu.VMEM((2, hb, CHUNK, 2 * HEAD_DIM), BF16),
            pltpu.VMEM((2, hb, CHUNK, CHUNK), BF16),
            pltpu.VMEM((2, hb, CHUNK, HEAD_DIM), BF16),
            pltpu.VMEM((2, hb, CHUNK, HEAD_DIM), BF16),
            pltpu.VMEM((2, hb, 1, HEAD_DIM), F32)]


N_HGRN_IN, N_DELTA_IN = 6, 8
N_HGRN_SCRATCH, N_DELTA_SCRATCH = 4, 10


def _mixers_kernel(*refs, n_chunks):
    c_len, hb = CHUNK, N_HEADS
    hgrn_in, refs = refs[:N_HGRN_IN], refs[N_HGRN_IN:]
    delta_in, refs = refs[:N_DELTA_IN], refs[N_DELTA_IN:]
    (oa_ref, ob_ref), refs = refs[:2], refs[2:]
    hgrn_scr, delta_scr = refs[:N_HGRN_SCRATCH], refs[N_HGRN_SCRATCH:]
    first_block = pl.program_id(1) == 0
    h_a, h_b = _hgrn_phases(*hgrn_in, oa_ref, *hgrn_scr, first_block)
    gates_of, d_a, d_b, save_history = _delta_phases(*delta_in, ob_ref, *delta_scr, first_block, n_chunks)
    heads = range(hb)

    def first_chunk(c, carry):
        r0 = pl.multiple_of(c * c_len, c_len)
        gates = gates_of(r0)
        _run_lockstep([d_a(hh, c, r0, c % 2, gates) for hh in heads]
                      + [h_a(hh, c, r0, c % 2) for hh in heads])
        return carry

    def chunk(c, carry):
        r0 = pl.multiple_of(c * c_len, c_len)
        prev = pl.multiple_of(r0 - c_len, c_len)
        slot = c % 2
        gates = gates_of(r0)
        _run_lockstep([d_b(hh, prev, 1 - slot) for hh in heads]
                      + [h_b(hh, prev, 1 - slot) for hh in heads]
                      + [d_a(hh, c, r0, slot, gates) for hh in heads]
                      + [h_a(hh, c, r0, slot) for hh in heads])
        return carry

    lax.fori_loop(0, 1, first_chunk, 0)
    lax.fori_loop(1, n_chunks, chunk, 0)
    last, last_slot = (n_chunks - 1) * c_len, (n_chunks - 1) % 2
    _run_lockstep([d_b(hh, last, last_slot) for hh in heads] + [h_b(hh, last, last_slot) for hh in heads])
    save_history()


def _mixers(proj, small, lb, g_norm_a, conv_w, scal, g_norm_b, *, bsz, seq, sb=512):
    sb = min(sb, seq)
    hb = N_HEADS
    wb = hb * HEAD_DIM
    col = lambda k: pl.BlockSpec((None, sb, wb), lambda b, s: (b, s, k))
    const = lambda shape: pl.BlockSpec(shape, lambda b, s: (0,) * len(shape))
    out = jax.ShapeDtypeStruct((bsz, seq, wb), BF16)
    return pl.pallas_call(
        functools.partial(_mixers_kernel, n_chunks=sb // CHUNK),
        out_shape=(out, out),
        grid=(bsz, seq // sb),
        in_specs=[col(0), col(1), col(2), col(3), const((hb, 1, HEAD_DIM)), const((1, HEAD_DIM)),
                  col(4), col(5), col(6), col(7),
                  pl.BlockSpec((None, sb, HEAD_DIM), lambda b, s: (b, s, 0)),
                  const((hb, 3, CONV_WIDTH, HEAD_DIM)), const((2, HEAD_DIM)), const((1, HEAD_DIM))],
        out_specs=(col(0), col(0)),
        scratch_shapes=_hgrn_scratch(hb) + _delta_scratch(hb),
        compiler_params=_params("parallel", "arbitrary"),
        name="mixers",
    )(proj, proj, proj, proj, lb, g_norm_a, proj, proj, proj, proj, small, conv_w, scal, g_norm_b)


def _cmul(ar, ai, br, bi):
    return ar * br - ai * bi, ar * bi + ai * br


def _s5_expand(a, g8):
    gs, lanes = S5_GROUP, S5_PACK * S5_GROUP
    sel = (lax.broadcasted_iota(jnp.int32, (a.shape[0], lanes), 1) // gs) == g8
    pieces = []
    for k in range(S5_L):
        src = a[:, (k // S5_PACK) * lanes:(k // S5_PACK + 1) * lanes]
        shift = ((g8 - k % S5_PACK + S5_PACK) * gs) % lanes
        pieces.append(jnp.where(sel, pltpu.roll(src, shift, axis=1), 0.0))
    return jnp.concatenate(pieces, axis=1)


def _s5_prep_kernel(are_ref, aim_ref, ldt_ref, bre_ref, bim_ref, cre_ref, cim_ref,
                    are2_ref, aim2_ref, ldt2_ref, w8_ref, m1_ref, m2_ref, lam_ref):
    def one_group(g8, carry):
        _s5_prep_group(g8, are_ref[g8], aim_ref[g8], ldt_ref[g8], bre_ref[g8], bim_ref[g8],
                       cre_ref[g8], cim_ref[g8], are2_ref[g8], aim2_ref[g8], ldt2_ref[g8],
                       w8_ref, m1_ref, m2_ref, lam_ref)
        return carry

    lax.fori_loop(0, S5_PACK, one_group, 0)


def _s5_prep_group(g8, are, aim, ldt, bre, bim, cre, cim, are2, aim2, ldt2,
                   w8_ref, m1_ref, m2_ref, lam_ref):
    p, w, l, gs = S5_STATE, S5_W, S5_L, S5_GROUP
    lanes = S5_PACK * gs
    hp = lax.Precision.HIGHEST
    dt = jnp.exp(ldt)
    a_re = jnp.broadcast_to(are, (p, w))
    a_im = jnp.broadcast_to(aim, (p, w))

    e_lane = jnp.minimum(lax.broadcasted_iota(jnp.int32, (p, lanes), 1), l).astype(F32)
    lr, li = e_lane * (are * dt), e_lane * (aim * dt)
    mag = jnp.exp(lr)
    powers = jnp.concatenate([mag * jnp.cos(li), mag * jnp.sin(li)], axis=0)
    sel_e = lax.broadcasted_iota(jnp.int32, (lanes, w), 0)
    sel_k = lax.broadcasted_iota(jnp.int32, (lanes, w), 1) // gs
    exps = (lambda k: k, lambda k: l - 1 - k, lambda k: k + 1)
    sel = jnp.concatenate([(sel_e == f(sel_k)).astype(BF16) for f in exps], axis=1)
    tiled, rest = 0.0, powers
    for _ in range(3):
        part = rest.astype(BF16)
        tiled = tiled + _dot(part, sel)
        rest = rest - part.astype(F32)

    def lam_pow(i):
        return tiled[:p, i * w:(i + 1) * w], tiled[p:, i * w:(i + 1) * w]

    lb_re = jnp.broadcast_to(powers[:p, 1:2], (p, w))
    lb_im = jnp.broadcast_to(powers[p:, 1:2], (p, w))
    den = a_re * a_re + a_im * a_im
    xr, xi = lb_re - 1.0, lb_im
    coef_re, coef_im = (xr * a_re + xi * a_im) / den, (xi * a_re - xr * a_im) / den
    bb_re, bb_im = _cmul(coef_re, coef_im, bre, bim)

    e_re, e_im = _cmul(*lam_pow(0), cre, cim)
    lhs = jnp.concatenate([bb_re[:, :gs], -bb_im[:, :gs]], axis=0)
    rhs = jnp.concatenate([e_re, e_im], axis=0)
    r0 = lax.dot_general(lhs, rhs, (((0,), (0,)), ((), ())), precision=hp,
                         preferred_element_type=F32)
    r0x = _s5_expand(r0, g8).astype(w8_ref.dtype)
    for s in range(l):
        rows = pl.ds(pl.multiple_of(s * lanes + g8 * gs, gs), gs)
        if s:
            w8_ref[rows, :s * lanes] = jnp.zeros((gs, s * lanes), w8_ref.dtype)
        w8_ref[rows, s * lanes:] = r0x[:, :(l - s) * lanes]

    st_rows = pl.ds(pl.multiple_of(g8 * 2 * p, 2 * p), 2 * p)
    d_re, d_im = _cmul(*lam_pow(1), bb_re, bb_im)
    m1_ref[st_rows, :] = _s5_expand(jnp.concatenate([d_re, d_im], axis=0), g8).astype(m1_ref.dtype)
    f_re, f_im = _cmul(*lam_pow(2), cre, cim)
    m2_ref[st_rows, :] = _s5_expand(jnp.concatenate([f_re, -f_im], axis=0), g8).astype(m2_ref.dtype)

    dt2 = jnp.exp(ldt2)
    mag_l = jnp.exp(float(l) * are2 * dt2)
    ang = float(l) * aim2 * dt2
    ll_re, ll_im = mag_l * jnp.cos(ang), mag_l * jnp.sin(ang)
    first = lax.broadcasted_iota(jnp.int32, ll_im.shape, 1) < p
    lam_ref[g8] = jnp.concatenate([ll_re, jnp.where(first, -ll_im, ll_im)]
                                  + [jnp.zeros_like(ll_re)] * 6, axis=0)


def _s5_prep(a_re, a_im, log_dt, b_re, b_im, c_re, c_im):
    g, p = a_re.shape
    w, l, pk = S5_W, S5_L, S5_PACK
    wx = l * pk * S5_GROUP
    col = lambda x: x.reshape(g, p, 1)
    tile = lambda x: jnp.tile(x, (1, 1, l))
    dup = lambda x: jnp.concatenate([x, x], axis=-1).reshape(g, 1, 2 * p)
    ldt2 = jnp.broadcast_to(log_dt.reshape(g, 1, 1), (g, 1, 2 * p))
    args = (col(a_re), col(a_im), log_dt.reshape(g, 1, 1), tile(b_re), tile(b_im),
            tile(jnp.swapaxes(c_re, 1, 2)), tile(jnp.swapaxes(c_im, 1, 2)),
            dup(a_re), dup(a_im), ldt2)

    def gspec(shape):
        return pl.BlockSpec((pk,) + shape, lambda i: (i,) + (0,) * len(shape))

    def ospec(shape):
        return pl.BlockSpec((None,) + shape, lambda i: (i,) + (0,) * len(shape))

    return pl.pallas_call(
        _s5_prep_kernel,
        out_shape=(jax.ShapeDtypeStruct((g // pk, wx, wx), BF16),
                   jax.ShapeDtypeStruct((g // pk, pk * 2 * p, wx), BF16),
                   jax.ShapeDtypeStruct((g // pk, pk * 2 * p, wx), BF16),
                   jax.ShapeDtypeStruct((g, 8, 2 * p), F32)),
        grid=(g // pk,),
        in_specs=[gspec((p, 1)), gspec((p, 1)), gspec((1, 1)), gspec((p, w)), gspec((p, w)),
                  gspec((p, w)), gspec((p, w)), gspec((1, 2 * p)), gspec((1, 2 * p)),
                  gspec((1, 2 * p))],
        out_specs=(ospec((wx, wx)), ospec((pk * 2 * p, wx)), ospec((pk * 2 * p, wx)),
                   gspec((8, 2 * p))),
        compiler_params=_params("parallel"),
        name="s5_prep",
    )(*args)


def _s5_gather(u_ref, x8_ref, nbk):
    lanes = u_ref.shape[1]
    for t in range(S5_L):
        x8_ref[:, t * lanes:(t + 1) * lanes] = u_ref[pl.ds(t, nbk, stride=S5_L), :].astype(BF16)


def _s5_inc_kernel(u_ref, m1_ref, inc_ref, x8_ref):
    _s5_gather(u_ref, x8_ref, inc_ref.shape[0])
    inc_ref[...] = _dot_nt(x8_ref[...], m1_ref[...])


def _s5_inc(u, m1, *, rb=2):
    t, d = u.shape
    ngb, sp, wx = m1.shape
    lanes = d // ngb
    rows = t // rb
    nbk = rows // S5_L
    return pl.pallas_call(
        _s5_inc_kernel,
        out_shape=jax.ShapeDtypeStruct((t // S5_L, ngb * sp), F32),
        grid=(ngb, rb),
        in_specs=[pl.BlockSpec((rows, lanes), lambda i, r: (r, i)),
                  pl.BlockSpec((None, sp, wx), lambda i, r: (i, 0, 0))],
        out_specs=pl.BlockSpec((nbk, sp), lambda i, r: (r, i)),
        scratch_shapes=[pltpu.VMEM((nbk, wx), BF16)],
        compiler_params=_params("parallel", "arbitrary"),
        name="s5_inc",
    )(u, m1)


def _s5_scan_kernel(inc_ref, lam_ref, x_ref, *, bsz, n_steps):
    lam = lam_ref[...]
    a, bc = lam[0:1, :], lam[1:2, :]
    width = inc_ref.shape[1]
    sp = 2 * S5_STATE
    per = SUBLANES // bsz
    first = (lax.broadcasted_iota(jnp.int32, (SUBLANES, width), 1) % sp) < S5_STATE
    row = lax.broadcasted_iota(jnp.int32, (SUBLANES, width), 0)

    def swap(v):
        return jnp.where(first, pltpu.roll(v, width - S5_STATE, axis=1),
                         pltpu.roll(v, S5_STATE, axis=1))

    def tile_step(m, carry):
        x, xs = carry
        rows = pl.ds(pl.multiple_of(m * SUBLANES, SUBLANES), SUBLANES)
        inc_tile = inc_ref[rows, :]
        out = x
        for j in range(per):
            inc = inc_tile if j == 0 else pltpu.roll(inc_tile, SUBLANES - j * bsz, axis=0)
            x, xs = a * x + bc * xs + inc, a * xs - bc * x + swap(inc)
            if j + 1 < per:
                out = jnp.where(row < (j + 1) * bsz, out, pltpu.roll(x, (j + 1) * bsz, axis=0))
        x_ref[rows, :] = out
        return x, xs

    zero = jnp.zeros((SUBLANES, width), F32)
    lax.fori_loop(0, n_steps // per, tile_step, (zero, zero), unroll=4)


def _s5_scan(inc, lam_rows, *, bsz, wb=2048):
    n, width = inc.shape
    blk = pl.BlockSpec((n, wb), lambda i: (0, i))
    return pl.pallas_call(
        functools.partial(_s5_scan_kernel, bsz=bsz, n_steps=n // bsz),
        out_shape=jax.ShapeDtypeStruct((n, width), F32),
        grid=(width // wb,),
        in_specs=[blk, pl.BlockSpec((8, wb), lambda i: (0, i))],
        out_specs=blk,
        compiler_params=_params("parallel"),
        name="s5_scan",
    )(inc, lam_rows)


def _s5_out_kernel(u_ref, x_ref, w8_ref, m2_ref, d_ref, o_ref, x8_ref, xs_ref):
    nbk = x_ref.shape[0]
    lanes = u_ref.shape[1]
    gelu_c = math.sqrt(2.0 / math.pi)
    d_skip = d_ref[...]
    _s5_gather(u_ref, x8_ref, nbk)
    xs_ref[...] = x_ref[...].astype(BF16)
    pair = 2 * lanes
    for t2 in range(S5_L // 2):
        cols = slice(t2 * pair, (t2 + 1) * pair)
        k = (t2 + 1) * pair
        y2 = _dot(x8_ref[:, :k], w8_ref[:k, cols]) + _dot(xs_ref[...], m2_ref[:, cols])
        for t in (2 * t2, 2 * t2 + 1):
            rows = pl.ds(t, nbk, stride=S5_L)
            y = y2[:, (t % 2) * lanes:(t % 2 + 1) * lanes] + d_skip * u_ref[rows, :]
            o_ref[rows, :] = 0.5 * y * (1.0 + jnp.tanh(gelu_c * (y + 0.044715 * (y * y * y))))


def _s5_out(u, x_all, w8, m2, d_skip, *, rb=2):
    t, d = u.shape
    ngb, sp, wx = m2.shape
    lanes = d // ngb
    rows = t // rb
    nbk = rows // S5_L
    return pl.pallas_call(
        _s5_out_kernel,
        out_shape=jax.ShapeDtypeStruct((t, d), F32),
        grid=(ngb, rb),
        in_specs=[pl.BlockSpec((rows, lanes), lambda i, r: (r, i)),
                  pl.BlockSpec((nbk, sp), lambda i, r: (r, i)),
                  pl.BlockSpec((None, wx, wx), lambda i, r: (i, 0, 0)),
                  pl.BlockSpec((None, sp, wx), lambda i, r: (i, 0, 0)),
                  pl.BlockSpec((1, lanes), lambda i, r: (0, i))],
        out_specs=pl.BlockSpec((rows, lanes), lambda i, r: (r, i)),
        scratch_shapes=[pltpu.VMEM((nbk, wx), BF16), pltpu.VMEM((nbk, sp), BF16)],
        compiler_params=_params("parallel", "arbitrary"),
        name="s5_out",
    )(u, x_all, w8, m2, d_skip)


def _s5_act(u, a_re, a_im, b_re, b_im, c_re, c_im, d_skip, log_dt, *, bsz):
    d = u.shape[1]
    g = d // S5_GROUP
    w8, m1, m2, lam = _s5_prep(a_re, a_im, log_dt, b_re, b_im, c_re, c_im)
    lam_rows = jnp.swapaxes(lam, 0, 1).reshape(8, g * 2 * S5_STATE)
    inc = _s5_inc(u, m1)
    x_all = _s5_scan(inc, lam_rows, bsz=bsz)
    return _s5_out(u, x_all, w8, m2, d_skip.reshape(1, d).astype(F32))


def kernel(x, p, norm_mix, norm_mlp, norm_ple, w_in_e, w_out_e, hgrn_lb, g_norm_a, conv_w, a_log, dt_bias, g_norm_b, s5_a_re, s5_a_im, s5_b_re, s5_b_im, s5_c_re, s5_c_im, s5_d, s5_log_dt, w_glu, b_glu, w_out_o, w_up, w_down, w_ple_gate, w_ple_proj, final_norm):
    bsz, seq, d = x.shape
    t = bsz * seq
    depth = p.shape[0]
    heads, hd = N_HEADS, HEAD_DIM
    width = heads * hd
    main_cols = 8 * width
    lower_bounds = jnp.cumsum(jax.nn.softmax(hgrn_lb.astype(F32), axis=0), axis=0)
    row = lambda v: v.reshape(1, -1).astype(F32)

    h = x.reshape(t, d)
    p_bf = p.reshape(depth, t, -1).astype(BF16)
    w_in_bf, w_out_e_bf, w_out_o_bf, w_glu_bf = (w.astype(BF16) for w in (w_in_e, w_out_e, w_out_o, w_glu))
    w_up_bf, w_down_bf, w_gate_bf, w_proj_bf = (w.astype(BF16) for w in (w_up, w_down, w_ple_gate, w_ple_proj))
    out = None
    for i in range(depth):
        j = i // 2
        if i % 2 == 0:
            w_small = jnp.pad(w_in_e[j, :, main_cols:], ((0, 0), (0, hd - 2 * heads))).astype(BF16)
            proj, small = _in_proj(h, row(norm_mix[i]), w_in_bf, j, main_cols, w_small)
            proj = proj.reshape(bsz, seq, main_cols)
            small = small.reshape(bsz, seq, hd)
            cw = conv_w[j].reshape(CONV_WIDTH, 3, heads, hd).transpose(2, 1, 0, 3)
            scal = jnp.pad(jnp.stack([a_log[j], dt_bias[j]]).astype(F32), ((0, 0), (0, hd - heads)))
            o_a, o_b = _mixers(proj, small, lower_bounds[i].reshape(heads, 1, hd), row(g_norm_a[j]),
                               cw, scal, row(g_norm_b[j]), bsz=bsz, seq=seq)
            mix_in = ([o_a.reshape(t, width), o_b.reshape(t, width)], w_out_e_bf, j)
            mix_seq = None
        else:
            act = _s5_act(normed.reshape(t, d), s5_a_re[j], s5_a_im[j], s5_b_re[j], s5_b_im[j],
                          s5_c_re[j], s5_c_im[j], s5_d[j], s5_log_dt[j], bsz=bsz)
            glu = _glu(act, w_glu_bf, j, row(b_glu[j]))
            mix_in = ([glu.reshape(seq // S5_L, bsz, S5_L, d)], w_out_o_bf, j)
            mix_seq = seq
        h = _mix_mlp(*mix_in, h, row(norm_mlp[i]), w_up_bf, w_down_bf, i, chunk_major_seq=mix_seq)
        last = i == depth - 1
        feeds_s5 = not last and (i + 1) % 2 == 1
        h, normed = _ple(h, row(norm_ple[i]), w_gate_bf, p_bf, w_proj_bf,
                         row(final_norm if last else norm_mix[i + 1]), i,
                         chunk_major=(bsz, seq) if feeds_s5 else None)
        out = normed
    return out.reshape(bsz, seq, d)
```

```python
import functools
import math

import jax
import jax.numpy as jnp
from jax import lax
from jax.experimental import pallas as pl
from jax.experimental.pallas import tpu as pltpu

F32 = jnp.float32
BF16 = jnp.bfloat16

SUBLANES = 8
NORM_EPS = 1e-6
CHUNK = 64
HEAD_DIM = 128
N_HEADS = 8
CONV_WIDTH = 4
S5_GROUP = 16
S5_STATE = 64
S5_L = 16
S5_W = S5_L * S5_GROUP
S5_PACK = 128 // S5_GROUP
EXP_CLAMP = 80.0

VMEM_LIMIT = 56 * 1024 * 1024


def _sigmoid(x):
    return 0.5 * jnp.tanh(0.5 * x) + 0.5


def _silu(x):
    return x * _sigmoid(x)


def _rms(x, g):
    return x * lax.rsqrt(jnp.mean(x * x, axis=-1, keepdims=True) + NORM_EPS) * g


def _dot(a, b):
    return jnp.dot(a, b, preferred_element_type=F32)


def _dot_nt(a, b):
    return lax.dot_general(a, b, (((1,), (1,)), ((), ())), preferred_element_type=F32)


def _dot_tn(a, b):
    return lax.dot_general(a, b, (((0,), (0,)), ((), ())), preferred_element_type=F32)


def _dot_split(a_bf, x):
    hi = x.astype(BF16)
    lo = (x - hi.astype(F32)).astype(BF16)
    return _dot(a_bf, hi) + _dot(a_bf, lo)


def _params(*sem):
    return pltpu.CompilerParams(dimension_semantics=sem, vmem_limit_bytes=VMEM_LIMIT)


def _in_proj_kernel(*refs, n_cast):
    x_ref, g_ref, w_ref, ws_ref = refs[:4]
    cast_in = refs[4:4 + n_cast]
    o_ref, os_ref = refs[4 + n_cast:6 + n_cast]
    cast_out = refs[6 + n_cast:6 + 2 * n_cast]
    hn_ref = refs[6 + 2 * n_cast]

    @pl.when(pl.program_id(1) == 0)
    def _():
        hn = _rms(x_ref[...], g_ref[...]).astype(BF16)
        hn_ref[...] = hn
        os_ref[...] = _dot(hn, ws_ref[...])

    o_ref[...] = _dot(hn_ref[...], w_ref[...])
    for src, dst in zip(cast_in, cast_out):
        dst[...] = src[...].astype(BF16)


IN_PROJ_TM, IN_PROJ_TN = 1024, 1024


def _in_proj(x, g, w, layer, n, w_small, to_cast, *, tm=IN_PROJ_TM, tn=IN_PROJ_TN):
    t, d = x.shape
    tm = min(tm, t)
    ns = w_small.shape[1]
    n_j = n // tn
    steps = (t // tm) * n_j
    flat = [a.reshape(-1, a.shape[-1]) for a in to_cast]
    def slab_specs():
        return [pl.BlockSpec((a.shape[0] // steps, a.shape[1]), lambda i, j: (i * n_j + j, 0))
                for a in flat]

    outs = pl.pallas_call(
        functools.partial(_in_proj_kernel, n_cast=len(flat)),
        out_shape=(jax.ShapeDtypeStruct((t, n), F32), jax.ShapeDtypeStruct((t, ns), F32),
                   *(jax.ShapeDtypeStruct(a.shape, BF16) for a in flat)),
        grid=(t // tm, n_j),
        in_specs=[pl.BlockSpec((tm, d), lambda i, j: (i, 0)),
                  pl.BlockSpec((1, d), lambda i, j: (0, 0)),
                  pl.BlockSpec((None, d, tn), lambda i, j: (layer, 0, j)),
                  pl.BlockSpec((d, ns), lambda i, j: (0, 0)),
                  *slab_specs()],
        out_specs=(pl.BlockSpec((tm, tn), lambda i, j: (i, j)),
                   pl.BlockSpec((tm, ns), lambda i, j: (i, 0)),
                   *slab_specs()),
        scratch_shapes=[pltpu.VMEM((tm, d), BF16)],
        compiler_params=_params("parallel", "arbitrary"),
        name="in_proj",
    )(x, g, w, w_small, *flat)
    return outs[0], outs[1], [o.reshape(a.shape) for o, a in zip(outs[2:], to_cast)]


def _can_cast_in_slabs(a, steps):
    rows = a.size // a.shape[-1]
    return rows % steps == 0 and (rows // steps) % (2 * SUBLANES) == 0


def _chunk_major_spec(tm, width, seq, **kw):
    per_seq = seq // tm
    return pl.BlockSpec((tm // S5_L, None, S5_L, width),
                        lambda i, *_: (i % per_seq, i // per_seq, 0, 0), **kw)


def _resident(shape, layer=None, block=0):
    if layer is None:
        return pl.BlockSpec(shape, lambda *_: (0,) * len(shape), pipeline_mode=pl.Buffered(1))
    return pl.BlockSpec((None,) + tuple(shape), lambda *_: (layer, block, 0),
                        pipeline_mode=pl.Buffered(1))


def _mix_mlp_kernel(*refs, n_in):
    x_refs, w_refs = refs[:n_in], refs[n_in:2 * n_in]
    r_ref, g_ref, wu_ref, wd_ref, o_ref, hn_ref = refs[2 * n_in:]

    @pl.when(pl.program_id(1) == 0)
    def _():
        h = r_ref[...]
        for x_ref, w_ref in zip(x_refs, w_refs):
            h = h + _dot(x_ref[...].reshape(-1, x_ref.shape[-1]), w_ref[...])
        hn_ref[...] = _rms(h, g_ref[...]).astype(BF16)
        o_ref[...] = h

    a = jnp.maximum(_dot(hn_ref[...], wu_ref[...]), 0.0)
    o_ref[...] += _dot((a * a).astype(BF16), wd_ref[...])


def _mix_mlp(xs, w_mix, mix_layer, resid, g, w_up, w_down, layer, *, chunk_major_seq=None,
             tm=512, tf=1024):
    t, d = resid.shape
    f = w_up.shape[2]
    n_in = len(xs)
    kx = w_mix.shape[1] // n_in
    if chunk_major_seq is None:
        x_specs = [pl.BlockSpec((tm, x.shape[1]), lambda i, j: (i, 0)) for x in xs]
    else:
        x_specs = [_chunk_major_spec(tm, x.shape[-1], chunk_major_seq) for x in xs]
    return pl.pallas_call(
        functools.partial(_mix_mlp_kernel, n_in=n_in),
        out_shape=jax.ShapeDtypeStruct((t, d), F32),
        grid=(t // tm, f // tf),
        in_specs=(x_specs + [_resident((kx, d), mix_layer, k) for k in range(n_in)]
                  + [pl.BlockSpec((tm, d), lambda i, j: (i, 0)),
                     pl.BlockSpec((1, d), lambda i, j: (0, 0)),
                     pl.BlockSpec((None, d, tf), lambda i, j: (layer, 0, j)),
                     pl.BlockSpec((None, tf, d), lambda i, j: (layer, j, 0))]),
        out_specs=pl.BlockSpec((tm, d), lambda i, j: (i, 0)),
        scratch_shapes=[pltpu.VMEM((tm, d), BF16)],
        compiler_params=_params("parallel", "arbitrary"),
        name="mix_mlp",
    )(*xs, *([w_mix] * n_in), resid, g, w_up, w_down)


def _ple_kernel(h_ref, g_ref, wg_ref, p_ref, wp_ref, g2_ref, o_ref, on_ref):
    h = h_ref[...]
    gate = _sigmoid(_dot(_rms(h, g_ref[...]).astype(BF16), wg_ref[...]))
    h_new = h + gate * _dot(p_ref[...].astype(BF16), wp_ref[...])
    o_ref[...] = h_new
    on_ref[...] = _rms(h_new, g2_ref[...]).reshape(on_ref.shape)


def _ple(h, g, w_gate, p, w_proj, g_next, layer, *, chunk_major=None, tm=512):
    t, d = h.shape
    pd = p.shape[2]
    if chunk_major is None:
        normed_shape, normed_spec = (t, d), pl.BlockSpec((tm, d), lambda i: (i, 0))
    else:
        bsz, seq = chunk_major
        normed_shape, normed_spec = (seq // S5_L, bsz, S5_L, d), _chunk_major_spec(tm, d, seq)
    return pl.pallas_call(
        _ple_kernel,
        out_shape=(jax.ShapeDtypeStruct((t, d), F32), jax.ShapeDtypeStruct(normed_shape, F32)),
        grid=(t // tm,),
        in_specs=[pl.BlockSpec((tm, d), lambda i: (i, 0)),
                  _resident((1, d)),
                  _resident((d, d), layer),
                  pl.BlockSpec((None, tm, pd), lambda i: (layer, i, 0)),
                  _resident((pd, d), layer),
                  _resident((1, d))],
        out_specs=(pl.BlockSpec((tm, d), lambda i: (i, 0)), normed_spec),
        compiler_params=_params("parallel"),
        name="ple",
    )(h, g, w_gate, p, w_proj, g_next)


def _glu_kernel(a_ref, w_ref, b_ref, o_ref):
    a = a_ref[...]
    z = _dot(a.astype(BF16), w_ref[...]) + b_ref[...]
    o_ref[...] = (a * _sigmoid(z)).astype(BF16)


def _glu(act, w, layer, b, *, tm=512):
    t, d = act.shape
    return pl.pallas_call(
        _glu_kernel,
        out_shape=jax.ShapeDtypeStruct((t, d), BF16),
        grid=(t // tm,),
        in_specs=[pl.BlockSpec((tm, d), lambda i: (i, 0)), _resident((d, d), layer),
                  _resident((1, d))],
        out_specs=pl.BlockSpec((tm, d), lambda i: (i, 0)),
        compiler_params=_params("parallel"),
        name="glu",
    )(act, w, b)


def _row(x, t):
    return x[t:t + 1, :]


def _run_lockstep(gens):
    live = list(gens)
    while live:
        nxt = []
        for g in live:
            try:
                next(g)
                nxt.append(g)
            except StopIteration:
                pass
        live = nxt


def _hgrn_phases(q_ref, f_ref, i_ref, g_ref, lb_ref, gn_ref, o_ref, st_ref, sc_ref, ost_ref, vb_ref,
                 first_block):
    c_len, d = CHUNK, HEAD_DIM

    @pl.when(first_block)
    def _():
        st_ref[...] = jnp.zeros_like(st_ref)

    ri = lax.broadcasted_iota(jnp.int32, (c_len, d), 0)
    rt = lax.broadcasted_iota(jnp.int32, (c_len, c_len), 0)
    rs = lax.broadcasted_iota(jnp.int32, (c_len, c_len), 1)
    tril = (rs <= rt).astype(BF16)
    diag_mask = jnp.logical_and(rs <= rt, (rs // 16) == (rt // 16))
    gn = gn_ref[...]

    def phase_a(hh, c, r0, slot):
        del c
        rows = pl.ds(r0, c_len)
        cols = slice(hh * d, (hh + 1) * d)
        lb = lb_ref[hh]
        q = q_ref[rows, cols]
        forget = lb + (1.0 - lb) * _sigmoid(f_ref[rows, cols])
        k = 1.0 - forget
        v = i_ref[rows, cols].astype(BF16)
        cum = _dot_split(tril, jnp.log(forget))
        yield

        def side(valid, ref_row, sign, x):
            e = jnp.where(valid, sign * (cum - ref_row), 0.0)
            return jnp.where(valid, x * jnp.exp(e), 0.0)

        c31 = _row(cum, 31)
        ref_b = jnp.where(ri < 32, _row(cum, 15), _row(cum, 47))
        ref_d = jnp.where(ri < 16, _row(cum, 8),
                          jnp.where(ri < 32, _row(cum, 24),
                                    jnp.where(ri < 48, _row(cum, 40), _row(cum, 56))))
        hi16 = (ri % 32) >= 16
        q_b = side(hi16, ref_b, 1.0, q)
        k_b = side(jnp.logical_not(hi16), ref_b, -1.0, k)
        q_off = jnp.concatenate([side(ri >= 32, c31, 1.0, q),
                                 jnp.where(ri < 32, q_b, 0.0),
                                 jnp.where(ri >= 32, q_b, 0.0)], axis=1).astype(BF16)
        k_off = jnp.concatenate([side(ri < 32, c31, -1.0, k),
                                 jnp.where(ri < 32, k_b, 0.0),
                                 jnp.where(ri >= 32, k_b, 0.0)], axis=1).astype(BF16)
        q_d = (q * jnp.exp(jnp.minimum(cum - ref_d, EXP_CLAMP))).astype(BF16)
        k_d = (k * jnp.exp(jnp.minimum(ref_d - cum, EXP_CLAMP))).astype(BF16)
        s_off = _dot_nt(q_off, k_off)
        s_diag = _dot_nt(q_d, k_d)
        cum_end = _row(cum, c_len - 1)
        q_dec = (q * jnp.exp(cum)).astype(BF16)
        k_dec = (k * jnp.exp(cum_end - cum)).astype(BF16)
        st = st_ref[hh]
        out_st = _dot_nt(q_dec, st.astype(BF16))
        st_new = st * jnp.exp(cum_end) + _dot_tn(v, k_dec)
        yield
        st_ref[hh] = st_new
        sc_ref[slot, hh] = (s_off + jnp.where(diag_mask, s_diag, 0.0)).astype(BF16)
        ost_ref[slot, hh] = out_st
        vb_ref[slot, hh] = v

    def phase_b(hh, r0, slot):
        rows = pl.ds(r0, c_len)
        cols = slice(hh * d, (hh + 1) * d)
        out = _dot(sc_ref[slot, hh], vb_ref[slot, hh]) + ost_ref[slot, hh]
        yield
        out = _rms(out, gn) * _silu(g_ref[rows, cols])
        o_ref[rows, cols] = out.astype(o_ref.dtype)

    return phase_a, phase_b


def _hgrn_scratch(hb):
    return [pltpu.VMEM((hb, HEAD_DIM, HEAD_DIM), F32),
            pltpu.VMEM((2, hb, CHUNK, CHUNK), BF16),
            pltpu.VMEM((2, hb, CHUNK, HEAD_DIM), F32),
            pltpu.VMEM((2, hb, CHUNK, HEAD_DIM), BF16)]


def _delta_phases(q_ref, k_ref, v_ref, z_ref, ab_ref, cw_ref, sc_ref, gn_ref, o_ref,
                  st_ref, hist_ref, win_ref, td_ref, nm_ref, rhs_ref, in_ref, qd_ref, kd_ref, dec_ref,
                  first_block, n_chunks):
    c_len, d = CHUNK, HEAD_DIM
    head0 = 0

    @pl.when(first_block)
    def _():
        st_ref[...] = jnp.zeros_like(st_ref)
        hist_ref[...] = jnp.zeros_like(hist_ref)

    rt = lax.broadcasted_iota(jnp.int32, (c_len, c_len), 0)
    rs = lax.broadcasted_iota(jnp.int32, (c_len, c_len), 1)
    causal = rs <= rt
    strict = rs < rt
    tril = causal.astype(BF16)
    ones_cc = jnp.ones((c_len, c_len), BF16)
    eye = (rs == rt).astype(F32)
    bt, bs = rt // 16, rs // 16
    m_diag = jnp.logical_and(strict, bt == bs)
    m_l1 = jnp.logical_and(bt // 2 == bs // 2, bt == bs + 1)
    m_l2 = jnp.logical_and(bt >= 2, bs < 2)
    lane = lax.broadcasted_iota(jnp.int32, (c_len, d), 1)
    gn = gn_ref[...]

    def conv(ref, which, hh, c, r0):
        cols = slice(hh * d, (hh + 1) * d)
        cw = cw_ref[hh, which]
        prev0 = pl.multiple_of(jnp.maximum(r0 - 8, 0), 8)
        prev = jnp.where(c > 0, ref[pl.ds(prev0, 8), cols], hist_ref[which, :, cols])
        cur = ref[pl.ds(r0, c_len), cols]
        win_ref[which, hh, 0:8, :] = prev
        win_ref[which, hh, 8:, :] = cur
        acc = cur * cw[CONV_WIDTH - 1:CONV_WIDTH, :]
        for j in range(1, CONV_WIDTH):
            shifted = win_ref[which, hh, 8 - j:8 - j + c_len, :]
            acc = acc + shifted * cw[CONV_WIDTH - 1 - j:CONV_WIDTH - j, :]
        return _silu(acc)

    def mm(a, b):
        return _dot(a.astype(BF16), b.astype(BF16))

    def chunk_gates(r0):
        ab = ab_ref[pl.ds(r0, c_len), :]
        x = ab + sc_ref[1:2, :]
        softplus = jnp.maximum(x, 0.0) + jnp.log(1.0 + jnp.exp(-jnp.abs(x)))
        return -jnp.exp(sc_ref[0:1, :]) * softplus, _sigmoid(ab)

    def phase_a(hh, c, r0, slot, gates):
        cols = slice(hh * d, (hh + 1) * d)
        head = head0 + hh
        log_a_all, beta_all = gates
        q = conv(q_ref, 0, hh, c, r0)
        k = conv(k_ref, 1, hh, c, r0)
        v = conv(v_ref, 2, hh, c, r0)
        q = q * lax.rsqrt(jnp.sum(q * q, axis=-1, keepdims=True) + NORM_EPS) * (d ** -0.5)
        k = k * lax.rsqrt(jnp.sum(k * k, axis=-1, keepdims=True) + NORM_EPS)
        yield

        la_col = jnp.sum(jnp.where(lane == head, log_a_all, 0.0), axis=-1, keepdims=True)
        beta = jnp.sum(jnp.where(lane == head + N_HEADS, beta_all, 0.0), axis=-1, keepdims=True)
        log_a = jnp.broadcast_to(la_col, (c_len, d))
        cum = _dot_split(tril, log_a)
        cum_row = _dot_split(ones_cc, jnp.where(rs >= rt, log_a[:, :c_len], 0.0))
        k_bf = k.astype(BF16)
        k_beta = k * beta
        kk = _dot_nt(k_beta.astype(BF16), k_bf)
        qk = _dot_nt(q.astype(BF16), k_bf)
        yield
        decay = jnp.exp(jnp.where(causal, cum[:, :c_len] - cum_row, 0.0))
        n_mat = jnp.where(strict, kk * decay, 0.0)

        n_d = jnp.where(m_diag, n_mat, 0.0)
        p2 = mm(n_d, n_d)
        e_cum = jnp.exp(cum)
        cum_end = _row(cum, c_len - 1)
        rhs_ref[slot, hh] = jnp.concatenate([v * beta, k_beta * e_cum], axis=1).astype(BF16)
        qd_ref[slot, hh] = (q * e_cum).astype(BF16)
        kd_ref[slot, hh] = (k * jnp.exp(cum_end - cum)).astype(BF16)
        dec_ref[slot, hh] = jnp.exp(cum_end)
        in_ref[slot, hh] = jnp.where(causal, qk * decay, 0.0).astype(BF16)
        nm_ref[slot, hh] = n_mat
        yield
        p4 = mm(p2, p2)
        a12 = mm(eye - n_d, eye + p2)
        yield
        p8 = mm(p4, p4)
        yield
        a48 = mm(eye + p4, eye + p8)
        yield
        td_ref[slot, hh] = mm(a12, a48)

    def phase_b(hh, r0, slot):
        rows = pl.ds(r0, c_len)
        cols = slice(hh * d, (hh + 1) * d)
        t_d = td_ref[slot, hh]
        n_mat = nm_ref[slot, hh]
        x1 = mm(t_d, jnp.where(m_l1, n_mat, 0.0))
        yield
        t_32 = t_d - mm(x1, t_d)
        yield
        x2 = mm(t_32, jnp.where(m_l2, n_mat, 0.0))
        yield
        t_inv = t_32 - mm(x2, t_32)
        yield
        sol = _dot(t_inv.astype(BF16), rhs_ref[slot, hh])
        yield
        u, w = sol[:, :d], sol[:, d:]
        st = st_ref[hh]
        st_bf = st.astype(BF16)
        v_new = u - _dot(w.astype(BF16), st_bf)
        out_st = _dot(qd_ref[slot, hh], st_bf)
        yield
        v_new_bf = v_new.astype(BF16)
        out = out_st + _dot(in_ref[slot, hh], v_new_bf)
        st_ref[hh] = st * dec_ref[slot, hh] + _dot_tn(kd_ref[slot, hh], v_new_bf)
        yield
        out = _rms(out, gn) * _silu(z_ref[rows, cols])
        o_ref[rows, cols] = out.astype(o_ref.dtype)

    def save_history():
        last8 = pl.ds(n_chunks * c_len - 8, 8)
        hist_ref[0] = q_ref[last8, :]
        hist_ref[1] = k_ref[last8, :]
        hist_ref[2] = v_ref[last8, :]

    return chunk_gates, phase_a, phase_b, save_history


def _delta_scratch(hb):
    return [pltpu.VMEM((hb, HEAD_DIM, HEAD_DIM), F32),
            pltpu.VMEM((3, 8, hb * HEAD_DIM), F32),
            pltpu.VMEM((3, hb, 8 + CHUNK, HEAD_DIM), F32),
            pltpu.VMEM((2, hb, CHUNK, CHUNK), F32),
            pltpu.VMEM((2, hb, CHUNK, CHUNK), F32),
            pltpu.VMEM((2, hb, CHUNK, 2 * HEAD_DIM), BF16),
            pltpu.VMEM((2, hb, CHUNK, CHUNK), BF16),
            pltpu.VMEM((2, hb, CHUNK, HEAD_DIM), BF16),
            pltpu.VMEM((2, hb, CHUNK, HEAD_DIM), BF16),
            pltpu.VMEM((2, hb, 1, HEAD_DIM), F32)]


N_HGRN_IN, N_DELTA_IN = 6, 8
N_HGRN_SCRATCH, N_DELTA_SCRATCH = 4, 10


def _mixers_kernel(*refs, n_chunks):
    c_len, hb = CHUNK, N_HEADS
    hgrn_in, refs = refs[:N_HGRN_IN], refs[N_HGRN_IN:]
    delta_in, refs = refs[:N_DELTA_IN], refs[N_DELTA_IN:]
    (oa_ref, ob_ref), refs = refs[:2], refs[2:]
    hgrn_scr, delta_scr = refs[:N_HGRN_SCRATCH], refs[N_HGRN_SCRATCH:]
    first_block = pl.program_id(1) == 0
    h_a, h_b = _hgrn_phases(*hgrn_in, oa_ref, *hgrn_scr, first_block)
    gates_of, d_a, d_b, save_history = _delta_phases(*delta_in, ob_ref, *delta_scr, first_block, n_chunks)
    heads = range(hb)

    def first_chunk(c, carry):
        r0 = pl.multiple_of(c * c_len, c_len)
        gates = gates_of(r0)
        _run_lockstep([d_a(hh, c, r0, c % 2, gates) for hh in heads]
                      + [h_a(hh, c, r0, c % 2) for hh in heads])
        return carry

    def chunk(c, carry):
        r0 = pl.multiple_of(c * c_len, c_len)
        prev = pl.multiple_of(r0 - c_len, c_len)
        slot = c % 2
        gates = gates_of(r0)
        _run_lockstep([d_b(hh, prev, 1 - slot) for hh in heads]
                      + [h_b(hh, prev, 1 - slot) for hh in heads]
                      + [d_a(hh, c, r0, slot, gates) for hh in heads]
                      + [h_a(hh, c, r0, slot) for hh in heads])
        return carry

    lax.fori_loop(0, 1, first_chunk, 0)
    lax.fori_loop(1, n_chunks, chunk, 0)
    last, last_slot = (n_chunks - 1) * c_len, (n_chunks - 1) % 2
    _run_lockstep([d_b(hh, last, last_slot) for hh in heads] + [h_b(hh, last, last_slot) for hh in heads])
    save_history()


def _mixers(proj, small, lb, g_norm_a, conv_w, scal, g_norm_b, *, bsz, seq, sb=512):
    sb = min(sb, seq)
    hb = N_HEADS
    wb = hb * HEAD_DIM
    col = lambda k: pl.BlockSpec((None, sb, wb), lambda b, s: (b, s, k))
    const = lambda shape: pl.BlockSpec(shape, lambda b, s: (0,) * len(shape))
    out = jax.ShapeDtypeStruct((bsz, seq, wb), BF16)
    return pl.pallas_call(
        functools.partial(_mixers_kernel, n_chunks=sb // CHUNK),
        out_shape=(out, out),
        grid=(bsz, seq // sb),
        in_specs=[col(0), col(1), col(2), col(3), const((hb, 1, HEAD_DIM)), const((1, HEAD_DIM)),
                  col(4), col(5), col(6), col(7),
                  pl.BlockSpec((None, sb, HEAD_DIM), lambda b, s: (b, s, 0)),
                  const((hb, 3, CONV_WIDTH, HEAD_DIM)), const((2, HEAD_DIM)), const((1, HEAD_DIM))],
        out_specs=(col(0), col(0)),
        scratch_shapes=_hgrn_scratch(hb) + _delta_scratch(hb),
        compiler_params=_params("parallel", "arbitrary"),
        name="mixers",
    )(proj, proj, proj, proj, lb, g_norm_a, proj, proj, proj, proj, small, conv_w, scal, g_norm_b)


def _cmul(ar, ai, br, bi):
    return ar * br - ai * bi, ar * bi + ai * br


def _s5_expand(a, g8):
    gs, lanes = S5_GROUP, S5_PACK * S5_GROUP
    sel = (lax.broadcasted_iota(jnp.int32, (a.shape[0], lanes), 1) // gs) == g8
    pieces = []
    for k in range(S5_L):
        src = a[:, (k // S5_PACK) * lanes:(k // S5_PACK + 1) * lanes]
        shift = ((g8 - k % S5_PACK + S5_PACK) * gs) % lanes
        pieces.append(jnp.where(sel, pltpu.roll(src, shift, axis=1), 0.0))
    return jnp.concatenate(pieces, axis=1)


def _s5_prep_kernel(are_ref, aim_ref, ldt_ref, bre_ref, bim_ref, cre_ref, cim_ref,
                    are2_ref, aim2_ref, ldt2_ref, w8_ref, m1_ref, m2_ref, lam_ref):
    def one_group(g8, carry):
        _s5_prep_group(g8, are_ref[g8], aim_ref[g8], ldt_ref[g8], bre_ref[g8], bim_ref[g8],
                       cre_ref[g8], cim_ref[g8], are2_ref[g8], aim2_ref[g8], ldt2_ref[g8],
                       w8_ref, m1_ref, m2_ref, lam_ref)
        return carry

    lax.fori_loop(0, S5_PACK, one_group, 0)


def _s5_prep_group(g8, are, aim, ldt, bre, bim, cre, cim, are2, aim2, ldt2,
                   w8_ref, m1_ref, m2_ref, lam_ref):
    p, w, l, gs = S5_STATE, S5_W, S5_L, S5_GROUP
    lanes = S5_PACK * gs
    hp = lax.Precision.HIGHEST
    dt = jnp.exp(ldt)
    a_re = jnp.broadcast_to(are, (p, w))
    a_im = jnp.broadcast_to(aim, (p, w))

    e_lane = jnp.minimum(lax.broadcasted_iota(jnp.int32, (p, lanes), 1), l).astype(F32)
    lr, li = e_lane * (are * dt), e_lane * (aim * dt)
    mag = jnp.exp(lr)
    powers = jnp.concatenate([mag * jnp.cos(li), mag * jnp.sin(li)], axis=0)
    sel_e = lax.broadcasted_iota(jnp.int32, (lanes, w), 0)
    sel_k = lax.broadcasted_iota(jnp.int32, (lanes, w), 1) // gs
    exps = (lambda k: k, lambda k: l - 1 - k, lambda k: k + 1)
    sel = jnp.concatenate([(sel_e == f(sel_k)).astype(BF16) for f in exps], axis=1)
    tiled, rest = 0.0, powers
    for _ in range(3):
        part = rest.astype(BF16)
        tiled = tiled + _dot(part, sel)
        rest = rest - part.astype(F32)

    def lam_pow(i):
        return tiled[:p, i * w:(i + 1) * w], tiled[p:, i * w:(i + 1) * w]

    lb_re = jnp.broadcast_to(powers[:p, 1:2], (p, w))
    lb_im = jnp.broadcast_to(powers[p:, 1:2], (p, w))
    den = a_re * a_re + a_im * a_im
    xr, xi = lb_re - 1.0, lb_im
    coef_re, coef_im = (xr * a_re + xi * a_im) / den, (xi * a_re - xr * a_im) / den
    bb_re, bb_im = _cmul(coef_re, coef_im, bre, bim)

    e_re, e_im = _cmul(*lam_pow(0), cre, cim)
    lhs = jnp.concatenate([bb_re[:, :gs], -bb_im[:, :gs]], axis=0)
    rhs = jnp.concatenate([e_re, e_im], axis=0)
    r0 = lax.dot_general(lhs, rhs, (((0,), (0,)), ((), ())), precision=hp,
                         preferred_element_type=F32)
    r0x = _s5_expand(r0, g8).astype(w8_ref.dtype)
    for s in range(l):
        rows = pl.ds(pl.multiple_of(s * lanes + g8 * gs, gs), gs)
        if s:
            w8_ref[rows, :s * lanes] = jnp.zeros((gs, s * lanes), w8_ref.dtype)
        w8_ref[rows, s * lanes:] = r0x[:, :(l - s) * lanes]

    st_rows = pl.ds(pl.multiple_of(g8 * 2 * p, 2 * p), 2 * p)
    d_re, d_im = _cmul(*lam_pow(1), bb_re, bb_im)
    m1_ref[st_rows, :] = _s5_expand(jnp.concatenate([d_re, d_im], axis=0), g8).astype(m1_ref.dtype)
    f_re, f_im = _cmul(*lam_pow(2), cre, cim)
    m2_ref[st_rows, :] = _s5_expand(jnp.concatenate([f_re, -f_im], axis=0), g8).astype(m2_ref.dtype)

    dt2 = jnp.exp(ldt2)
    mag_l = jnp.exp(float(l) * are2 * dt2)
    ang = float(l) * aim2 * dt2
    ll_re, ll_im = mag_l * jnp.cos(ang), mag_l * jnp.sin(ang)
    first = lax.broadcasted_iota(jnp.int32, ll_im.shape, 1) < p
    lam_ref[g8] = jnp.concatenate([ll_re, jnp.where(first, -ll_im, ll_im)]
                                  + [jnp.zeros_like(ll_re)] * 6, axis=0)


def _s5_prep(a_re, a_im, log_dt, b_re, b_im, c_re, c_im):
    g, p = a_re.shape
    w, l, pk = S5_W, S5_L, S5_PACK
    wx = l * pk * S5_GROUP
    col = lambda x: x.reshape(g, p, 1)
    tile = lambda x: jnp.tile(x, (1, 1, l))
    dup = lambda x: jnp.concatenate([x, x], axis=-1).reshape(g, 1, 2 * p)
    ldt2 = jnp.broadcast_to(log_dt.reshape(g, 1, 1), (g, 1, 2 * p))
    args = (col(a_re), col(a_im), log_dt.reshape(g, 1, 1), tile(b_re), tile(b_im),
            tile(jnp.swapaxes(c_re, 1, 2)), tile(jnp.swapaxes(c_im, 1, 2)),
            dup(a_re), dup(a_im), ldt2)

    def gspec(shape):
        return pl.BlockSpec((pk,) + shape, lambda i: (i,) + (0,) * len(shape))

    def ospec(shape):
        return pl.BlockSpec((None,) + shape, lambda i: (i,) + (0,) * len(shape))

    return pl.pallas_call(
        _s5_prep_kernel,
        out_shape=(jax.ShapeDtypeStruct((g // pk, wx, wx), BF16),
                   jax.ShapeDtypeStruct((g // pk, pk * 2 * p, wx), BF16),
                   jax.ShapeDtypeStruct((g // pk, pk * 2 * p, wx), BF16),
                   jax.ShapeDtypeStruct((g, 8, 2 * p), F32)),
        grid=(g // pk,),
        in_specs=[gspec((p, 1)), gspec((p, 1)), gspec((1, 1)), gspec((p, w)), gspec((p, w)),
                  gspec((p, w)), gspec((p, w)), gspec((1, 2 * p)), gspec((1, 2 * p)),
                  gspec((1, 2 * p))],
        out_specs=(ospec((wx, wx)), ospec((pk * 2 * p, wx)), ospec((pk * 2 * p, wx)),
                   gspec((8, 2 * p))),
        compiler_params=_params("parallel"),
        name="s5_prep",
    )(*args)


def _s5_gather(u_ref, x8_ref, nbk):
    lanes = u_ref.shape[1]
    for t in range(S5_L):
        x8_ref[:, t * lanes:(t + 1) * lanes] = u_ref[pl.ds(t, nbk, stride=S5_L), :].astype(BF16)


def _s5_inc_kernel(u_ref, m1_ref, inc_ref, x8_ref):
    _s5_gather(u_ref, x8_ref, inc_ref.shape[0])
    inc_ref[...] = _dot_nt(x8_ref[...], m1_ref[...])


def _s5_inc(u, m1, *, rb=2):
    t, d = u.shape
    ngb, sp, wx = m1.shape
    lanes = d // ngb
    rows = t // rb
    nbk = rows // S5_L
    return pl.pallas_call(
        _s5_inc_kernel,
        out_shape=jax.ShapeDtypeStruct((t // S5_L, ngb * sp), F32),
        grid=(ngb, rb),
        in_specs=[pl.BlockSpec((rows, lanes), lambda i, r: (r, i)),
                  pl.BlockSpec((None, sp, wx), lambda i, r: (i, 0, 0))],
        out_specs=pl.BlockSpec((nbk, sp), lambda i, r: (r, i)),
        scratch_shapes=[pltpu.VMEM((nbk, wx), BF16)],
        compiler_params=_params("parallel", "arbitrary"),
        name="s5_inc",
    )(u, m1)


def _s5_scan_kernel(inc_ref, lam_ref, x_ref, *, bsz, n_steps):
    lam = lam_ref[...]
    a, bc = lam[0:1, :], lam[1:2, :]
    width = inc_ref.shape[1]
    sp = 2 * S5_STATE
    per = SUBLANES // bsz
    first = (lax.broadcasted_iota(jnp.int32, (SUBLANES, width), 1) % sp) < S5_STATE
    row = lax.broadcasted_iota(jnp.int32, (SUBLANES, width), 0)

    def swap(v):
        return jnp.where(first, pltpu.roll(v, width - S5_STATE, axis=1),
                         pltpu.roll(v, S5_STATE, axis=1))

    def tile_step(m, carry):
        x, xs = carry
        rows = pl.ds(pl.multiple_of(m * SUBLANES, SUBLANES), SUBLANES)
        inc_tile = inc_ref[rows, :]
        out = x
        for j in range(per):
            inc = inc_tile if j == 0 else pltpu.roll(inc_tile, SUBLANES - j * bsz, axis=0)
            x, xs = a * x + bc * xs + inc, a * xs - bc * x + swap(inc)
            if j + 1 < per:
                out = jnp.where(row < (j + 1) * bsz, out, pltpu.roll(x, (j + 1) * bsz, axis=0))
        x_ref[rows, :] = out
        return x, xs

    zero = jnp.zeros((SUBLANES, width), F32)
    lax.fori_loop(0, n_steps // per, tile_step, (zero, zero), unroll=4)


def _s5_scan(inc, lam_rows, *, bsz, wb=2048):
    n, width = inc.shape
    blk = pl.BlockSpec((n, wb), lambda i: (0, i))
    return pl.pallas_call(
        functools.partial(_s5_scan_kernel, bsz=bsz, n_steps=n // bsz),
        out_shape=jax.ShapeDtypeStruct((n, width), F32),
        grid=(width // wb,),
        in_specs=[blk, pl.BlockSpec((8, wb), lambda i: (0, i))],
        out_specs=blk,
        compiler_params=_params("parallel"),
        name="s5_scan",
    )(inc, lam_rows)


def _s5_out_kernel(u_ref, x_ref, w8_ref, m2_ref, d_ref, o_ref, x8_ref, xs_ref):
    nbk = x_ref.shape[0]
    lanes = u_ref.shape[1]
    gelu_c = math.sqrt(2.0 / math.pi)
    d_skip = d_ref[...]
    _s5_gather(u_ref, x8_ref, nbk)
    xs_ref[...] = x_ref[...].astype(BF16)
    pair = 2 * lanes
    for t2 in range(S5_L // 2):
        cols = slice(t2 * pair, (t2 + 1) * pair)
        k = (t2 + 1) * pair
        y2 = _dot(x8_ref[:, :k], w8_ref[:k, cols]) + _dot(xs_ref[...], m2_ref[:, cols])
        for t in (2 * t2, 2 * t2 + 1):
            rows = pl.ds(t, nbk, stride=S5_L)
            y = y2[:, (t % 2) * lanes:(t % 2 + 1) * lanes] + d_skip * u_ref[rows, :]
            o_ref[rows, :] = 0.5 * y * (1.0 + jnp.tanh(gelu_c * (y + 0.044715 * (y * y * y))))


def _s5_out(u, x_all, w8, m2, d_skip, *, rb=2):
    t, d = u.shape
    ngb, sp, wx = m2.shape
    lanes = d // ngb
    rows = t // rb
    nbk = rows // S5_L
    return pl.pallas_call(
        _s5_out_kernel,
        out_shape=jax.ShapeDtypeStruct((t, d), F32),
        grid=(ngb, rb),
        in_specs=[pl.BlockSpec((rows, lanes), lambda i, r: (r, i)),
                  pl.BlockSpec((nbk, sp), lambda i, r: (r, i)),
                  pl.BlockSpec((None, wx, wx), lambda i, r: (i, 0, 0)),
                  pl.BlockSpec((None, sp, wx), lambda i, r: (i, 0, 0)),
                  pl.BlockSpec((1, lanes), lambda i, r: (0, i))],
        out_specs=pl.BlockSpec((rows, lanes), lambda i, r: (r, i)),
        scratch_shapes=[pltpu.VMEM((nbk, wx), BF16), pltpu.VMEM((nbk, sp), BF16)],
        compiler_params=_params("parallel", "arbitrary"),
        name="s5_out",
    )(u, x_all, w8, m2, d_skip)


def _s5_act(u, a_re, a_im, b_re, b_im, c_re, c_im, d_skip, log_dt, *, bsz):
    d = u.shape[1]
    g = d // S5_GROUP
    w8, m1, m2, lam = _s5_prep(a_re, a_im, log_dt, b_re, b_im, c_re, c_im)
    lam_rows = jnp.swapaxes(lam, 0, 1).reshape(8, g * 2 * S5_STATE)
    inc = _s5_inc(u, m1)
    x_all = _s5_scan(inc, lam_rows, bsz=bsz)
    return _s5_out(u, x_all, w8, m2, d_skip.reshape(1, d).astype(F32))


def kernel(x, p, norm_mix, norm_mlp, norm_ple, w_in_e, w_out_e, hgrn_lb, g_norm_a, conv_w, a_log, dt_bias, g_norm_b, s5_a_re, s5_a_im, s5_b_re, s5_b_im, s5_c_re, s5_c_im, s5_d, s5_log_dt, w_glu, b_glu, w_out_o, w_up, w_down, w_ple_gate, w_ple_proj, final_norm):
    bsz, seq, d = x.shape
    t = bsz * seq
    depth = p.shape[0]
    heads, hd = N_HEADS, HEAD_DIM
    width = heads * hd
    main_cols = 8 * width
    lower_bounds = jnp.cumsum(jax.nn.softmax(hgrn_lb.astype(F32), axis=0), axis=0)
    row = lambda v: v.reshape(1, -1).astype(F32)

    h = x.reshape(t, d)
    p_rows = p.reshape(depth, t, -1)
    w_in_bf, w_proj_bf = w_in_e.astype(BF16), w_ple_proj.astype(BF16)
    later = {"out_e": w_out_e, "out_o": w_out_o, "glu": w_glu, "up": w_up, "down": w_down,
             "gate": w_ple_gate}
    proj_steps = (t // min(IN_PROJ_TM, t)) * (main_cols // IN_PROJ_TN)
    in_kernel = [k for k, w in later.items() if _can_cast_in_slabs(w, proj_steps)]
    w_bf = {k: w.astype(BF16) for k, w in later.items() if k not in in_kernel}
    out = None
    for i in range(depth):
        j = i // 2
        if i % 2 == 0:
            w_small = jnp.pad(w_in_e[j, :, main_cols:], ((0, 0), (0, hd - 2 * heads))).astype(BF16)
            to_cast = [later[k] for k in in_kernel] if i == 0 else []
            proj, small, cast = _in_proj(h, row(norm_mix[i]), w_in_bf, j, main_cols, w_small, to_cast)
            if i == 0:
                w_bf.update(zip(in_kernel, cast))
            proj = proj.reshape(bsz, seq, main_cols)
            small = small.reshape(bsz, seq, hd)
            cw = conv_w[j].reshape(CONV_WIDTH, 3, heads, hd).transpose(2, 1, 0, 3)
            scal = jnp.pad(jnp.stack([a_log[j], dt_bias[j]]).astype(F32), ((0, 0), (0, hd - heads)))
            o_a, o_b = _mixers(proj, small, lower_bounds[i].reshape(heads, 1, hd), row(g_norm_a[j]),
                               cw, scal, row(g_norm_b[j]), bsz=bsz, seq=seq)
            mix_in = ([o_a.reshape(t, width), o_b.reshape(t, width)], w_bf["out_e"], j)
            mix_seq = None
        else:
            act = _s5_act(normed.reshape(t, d), s5_a_re[j], s5_a_im[j], s5_b_re[j], s5_b_im[j],
                          s5_c_re[j], s5_c_im[j], s5_d[j], s5_log_dt[j], bsz=bsz)
            glu = _glu(act, w_bf["glu"], j, row(b_glu[j]))
            mix_in = ([glu.reshape(seq // S5_L, bsz, S5_L, d)], w_bf["out_o"], j)
            mix_seq = seq
        h = _mix_mlp(*mix_in, h, row(norm_mlp[i]), w_bf["up"], w_bf["down"], i,
                     chunk_major_seq=mix_seq)
        last = i == depth - 1
        feeds_s5 = not last and (i + 1) % 2 == 1
        h, normed = _ple(h, row(norm_ple[i]), w_bf["gate"], p_rows, w_proj_bf,
                         row(final_norm if last else norm_mix[i + 1]), i,
                         chunk_major=(bsz, seq) if feeds_s5 else None)
        out = normed
    return out.reshape(bsz, seq, d)
```
